```python
import jax, jax.numpy as jnp
from jax import lax
import numpy as np

D_MODEL = 4096
BATCH = 8
SEQ = 2048
DEPTH = 4

N_A_LAYERS = DEPTH // 2
N_B_LAYERS = DEPTH - N_A_LAYERS
POOL_WINDOWS = (2, 4, 8, 16)
N_POOL_GROUPS = len(POOL_WINDOWS)
POOL_GROUP_DIM = D_MODEL // N_POOL_GROUPS
HEAD_DIM = 128
N_HEADS = D_MODEL // HEAD_DIM
Q_BLOCK = 128
D_FF = ((8 * D_MODEL // 3 + 255) // 256) * 256
CONV_WIDTH = 3
EPS = 1e-6

kernel_name = "yoco_pool_forgetting_attention_convffn"


def rms_norm(x, g):
    x32 = x.astype(jnp.float32)
    y = x32 * lax.rsqrt(jnp.mean(x32 * x32, axis=-1, keepdims=True) + EPS)
    return (y * g.astype(jnp.float32)).astype(x.dtype)


def causal_window_mean(u, w):
    S = u.shape[1]
    c = jnp.cumsum(u.astype(jnp.float32), axis=1)
    c = jnp.pad(c, ((0, 0), (1, 0), (0, 0)))
    lag = jnp.pad(c, ((0, 0), (w - 1, 0), (0, 0)))[:, :S]
    count = jnp.minimum(jnp.arange(S) + 1, w).astype(jnp.float32)[None, :, None]
    return (c[:, 1:] - lag) / count


def pool_mixer(h, w_pool, scale):
    B, S, D = h.shape
    hg = h.reshape(B, S, N_POOL_GROUPS, POOL_GROUP_DIM)
    pooled = jnp.stack(
        [causal_window_mean(hg[:, :, g], w) for g, w in enumerate(POOL_WINDOWS)], axis=2)
    mix = (pooled - hg.astype(jnp.float32)).astype(h.dtype)
    y = jnp.einsum('bsgc,gcd->bsgd', mix, w_pool).reshape(B, S, D)
    return y * scale


def conv_ffn(h, w_up, conv_w, conv_b, w_down):
    S = h.shape[1]
    u = h @ w_up
    up = jnp.pad(u, ((0, 0), (CONV_WIDTH - 1, 0), (0, 0)))
    y = conv_b
    for k in range(CONV_WIDTH):
        start = CONV_WIDTH - 1 - k
        y = y + conv_w[k] * up[:, start:start + S]
    gate, val = jnp.split(y, 2, axis=-1)
    return (jax.nn.silu(gate) * val) @ w_down


def shared_kv(h_kv, w_kvf, b_f):
    B, S, _ = h_kv.shape
    kvf = h_kv @ w_kvf
    k, v, f_logit = jnp.split(kvf, [D_MODEL, 2 * D_MODEL], axis=-1)
    k = k.reshape(B, S, N_HEADS, HEAD_DIM).transpose(0, 2, 1, 3)
    v = v.reshape(B, S, N_HEADS, HEAD_DIM).transpose(0, 2, 1, 3)
    log_f = jax.nn.log_sigmoid((f_logit + b_f).astype(jnp.float32))
    cum = jnp.cumsum(log_f, axis=1).transpose(0, 2, 1)
    return k, v, cum


def forgetting_attention(h, w_q, w_o, k, v, cum):
    B, S, _ = h.shape
    n_blocks = S // Q_BLOCK
    q = (h @ w_q).reshape(B, S, N_HEADS, HEAD_DIM).transpose(0, 2, 1, 3)
    qb = q.reshape(B, N_HEADS, n_blocks, Q_BLOCK, HEAD_DIM).transpose(2, 0, 1, 3, 4)
    cb = cum.reshape(B, N_HEADS, n_blocks, Q_BLOCK).transpose(2, 0, 1, 3)
    key_pos = jnp.arange(S)
    scale = HEAD_DIM ** -0.5
    neg = jnp.finfo(jnp.float32).min

    def block(args):
        i, q_i, c_i = args
        logits = jnp.einsum('bhqd,bhkd->bhqk', q_i, k).astype(jnp.float32) * scale
        logits = logits + c_i[..., :, None] - cum[..., None, :]
        q_pos = i * Q_BLOCK + jnp.arange(Q_BLOCK)
        causal = key_pos[None, :] <= q_pos[:, None]
        logits = jnp.where(causal, logits, neg)
        p = jax.nn.softmax(logits, axis=-1).astype(v.dtype)
        return jnp.einsum('bhqk,bhkd->bhqd', p, v)

    o = lax.map(block, (jnp.arange(n_blocks), qb, cb))
    o = o.transpose(1, 0, 3, 2, 4).reshape(B, S, D_MODEL)
    return o @ w_o


def setup_inputs(seed: int = 0) -> dict:
    key = jax.random.key(seed)
    ks = jax.random.split(key, 20)
    f32 = jnp.float32
    D, F, H = D_MODEL, D_FF, N_HEADS

    def nrm(k, shape, s):
        return jax.random.normal(k, shape, f32) * s

    return {
        "x": nrm(ks[0], (BATCH, SEQ, D), 1.0),
        "ln_mix": 1.0 + nrm(ks[1], (DEPTH, D), 0.02),
        "ln_ffn": 1.0 + nrm(ks[2], (DEPTH, D), 0.02),
        "pool_w": nrm(ks[3], (N_A_LAYERS, N_POOL_GROUPS, POOL_GROUP_DIM, POOL_GROUP_DIM), POOL_GROUP_DIM ** -0.5),
        "pool_scale": 0.5 + nrm(ks[4], (N_A_LAYERS, D), 0.1),
        "kv_norm": 1.0 + nrm(ks[5], (D,), 0.02),
        "w_kvf": nrm(ks[6], (D, 2 * D + H), D ** -0.5),
        "b_f": jax.random.uniform(ks[7], (H,), f32, 1.0, 5.0),
        "w_q": nrm(ks[8], (N_B_LAYERS, D, D), D ** -0.5),
        "w_o": nrm(ks[9], (N_B_LAYERS, D, D), D ** -0.5),
        "w_up": nrm(ks[10], (DEPTH, D, 2 * F), D ** -0.5),
        "conv_w": nrm(ks[11], (DEPTH, CONV_WIDTH, 2 * F), CONV_WIDTH ** -0.5),
        "conv_b": nrm(ks[12], (DEPTH, 2 * F), 0.01),
        "w_down": nrm(ks[13], (DEPTH, F, D), F ** -0.5),
        "final_norm": 1.0 + nrm(ks[14], (D,), 0.02),
    }


def reference(x, ln_mix, ln_ffn, pool_w, pool_scale, kv_norm, w_kvf, b_f,
              w_q, w_o, w_up, conv_w, conv_b, w_down, final_norm):
    h = x
    for layer in range(N_A_LAYERS):
        h = h + pool_mixer(rms_norm(h, ln_mix[layer]), pool_w[layer], pool_scale[layer])
        h = h + conv_ffn(rms_norm(h, ln_ffn[layer]), w_up[layer], conv_w[layer],
                         conv_b[layer], w_down[layer])
    k, v, cum = shared_kv(rms_norm(h, kv_norm), w_kvf, b_f)
    for j in range(N_B_LAYERS):
        layer = N_A_LAYERS + j
        h = h + forgetting_attention(rms_norm(h, ln_mix[layer]), w_q[j], w_o[j], k, v, cum)
        h = h + conv_ffn(rms_norm(h, ln_ffn[layer]), w_up[layer], conv_w[layer],
                         conv_b[layer], w_down[layer])
    return rms_norm(h, final_norm)
```

```python
import functools

import jax
import jax.numpy as jnp
from jax import lax
from jax.experimental import pallas as pl
from jax.experimental.pallas import tpu as pltpu

EPS = 1e-6
POOL_WINDOWS = (2, 4, 8, 16)
POOL_CARRY_ROWS = 16
HEAD_DIM = 128
CONV_WIDTH = 3
SUBLANES = 8
LANES = 128
V7X_VMEM_BYTES = 64 * 1024 * 1024
VMEM_RESERVE_BYTES = 6 * 1024 * 1024

F32 = jnp.float32
BF16 = jnp.bfloat16


def _round_up(x, m):
    return (x + m - 1) // m * m


def _largest_tile(n, cap, quantum):
    if n <= cap:
        return n
    t = cap - cap % quantum
    while t >= quantum:
        if n % t == 0:
            return t
        t -= quantum
    raise ValueError(f"no tile of {n} is a multiple of {quantum} and <= {cap}")


def _params(vmem_estimate_bytes, n_grid_dims):
    limit = min(int(vmem_estimate_bytes * 5 // 4) + VMEM_RESERVE_BYTES,
                V7X_VMEM_BYTES - VMEM_RESERVE_BYTES)
    return pltpu.CompilerParams(
        dimension_semantics=("arbitrary",) * n_grid_dims,
        vmem_limit_bytes=limit,
    )


def _rms(x, g):
    ms = jnp.mean(x * x, axis=-1, keepdims=True)
    return (x * lax.rsqrt(ms + EPS)) * g


def _pool_kernel(h_ref, gm_ref, gf_ref, sc_ref, w_ref, ho_ref, n2_ref, carry_ref):
    s = pl.program_id(1)
    ts, d = h_ref.shape
    c = d // len(POOL_WINDOWS)

    @pl.when(s == 0)
    def _():
        carry_ref[...] = jnp.zeros_like(carry_ref)

    x = h_ref[...]
    n = _rms(x, gm_ref[...])
    pos = s * ts + lax.broadcasted_iota(jnp.int32, (ts, 1), 0)
    for g, w in enumerate(POOL_WINDOWS):
        cols = slice(g * c, (g + 1) * c)
        acc = jnp.concatenate([carry_ref[:, cols], n[:, cols]], axis=0)
        shift = 1
        while shift < w:
            acc = acc + pltpu.roll(acc, shift, 0)
            shift *= 2
        count = jnp.minimum(pos + 1, w).astype(F32)
        mix = (acc[POOL_CARRY_ROWS:, :] / count - n[:, cols]).astype(BF16)
        y = jnp.dot(mix, w_ref[g], preferred_element_type=F32)
        ho_ref[:, cols] = x[:, cols] + y * sc_ref[:, cols]
    carry_ref[...] = n[ts - POOL_CARRY_ROWS:, :]
    n2_ref[...] = _rms(ho_ref[...], gf_ref[...]).astype(n2_ref.dtype)


def _pool_layer(h, g_mix, g_ffn, scale, w_pool):
    b, s, d = h.shape
    ts = _largest_tile(s, 128, POOL_CARRY_ROWS)
    vmem = 2 * ts * d * (4 + 4 + 2) + 2 * w_pool.size * 2 + 8 * ts * d * 4
    row = pl.BlockSpec((None, ts, d), lambda i, j: (i, j, 0))
    vec = pl.BlockSpec((1, d), lambda i, j: (0, 0))
    return pl.pallas_call(
        _pool_kernel,
        grid=(b, s // ts),
        in_specs=[row, vec, vec, vec, pl.BlockSpec(w_pool.shape, lambda i, j: (0, 0, 0))],
        out_specs=[row, row],
        out_shape=[jax.ShapeDtypeStruct(h.shape, F32), jax.ShapeDtypeStruct(h.shape, BF16)],
        scratch_shapes=[pltpu.VMEM((POOL_CARRY_ROWS, d), F32)],
        compiler_params=_params(vmem, 2),
        name="pool_layer",
    )(h, g_mix, g_ffn, scale, w_pool)


def _causal_conv(u, tail, w_ref, b_ref):
    rows = lax.broadcasted_iota(jnp.int32, tail.shape, 0)

    def lagged(k):
        r = pltpu.roll(u, k, 0)
        head = jnp.where(rows < k, pltpu.roll(tail, k, 0), r[:SUBLANES])
        return jnp.concatenate([head, r[SUBLANES:]], axis=0)

    y = b_ref[...] + w_ref[0:1, :] * u
    for k in range(1, CONV_WIDTH):
        y = y + w_ref[k:k + 1, :] * lagged(k)
    return y


def _ffn_up_kernel(a_ref, wg_ref, wv_ref, cwg_ref, cwv_ref, cbg_ref, cbv_ref, o_ref, *, rc):
    s, bn = o_ref.shape
    tail_g = jnp.zeros((SUBLANES, bn), F32)
    tail_v = jnp.zeros((SUBLANES, bn), F32)
    for r0 in range(0, s, rc):
        a = a_ref[r0:r0 + rc, :]
        ug = jnp.dot(a, wg_ref[...], preferred_element_type=F32)
        uv = jnp.dot(a, wv_ref[...], preferred_element_type=F32)
        yg = _causal_conv(ug, tail_g, cwg_ref, cbg_ref)
        yv = _causal_conv(uv, tail_v, cwv_ref, cbv_ref)
        o_ref[r0:r0 + rc, :] = (yg * jax.nn.sigmoid(yg) * yv).astype(o_ref.dtype)
        tail_g = ug[rc - SUBLANES:, :]
        tail_v = uv[rc - SUBLANES:, :]


def _ffn_up(n2, w_up, conv_w, conv_b):
    b, s, d = n2.shape
    fp = w_up.shape[1] // 2
    bn = _largest_tile(fp, 256, LANES)
    nj = fp // bn
    rc = _largest_tile(s, 512, SUBLANES)
    vmem = 2 * s * d * 2 + 4 * d * bn * 2 + 2 * s * bn * 2 + 12 * rc * bn * 4
    gate = lambda i, j: (0, j)
    val = lambda i, j: (0, j + nj)
    return pl.pallas_call(
        functools.partial(_ffn_up_kernel, rc=rc),
        grid=(b, nj),
        in_specs=[
            pl.BlockSpec((None, s, d), lambda i, j: (i, 0, 0)),
            pl.BlockSpec((d, bn), gate), pl.BlockSpec((d, bn), val),
            pl.BlockSpec((CONV_WIDTH, bn), gate), pl.BlockSpec((CONV_WIDTH, bn), val),
            pl.BlockSpec((1, bn), gate), pl.BlockSpec((1, bn), val),
        ],
        out_specs=pl.BlockSpec((None, s, bn), lambda i, j: (i, 0, j)),
        out_shape=jax.ShapeDtypeStruct((b, s, fp), BF16),
        compiler_params=_params(vmem, 2),
        name="ffn_up",
    )(n2, w_up, w_up, conv_w, conv_w, conv_b, conv_b)


def _mm_res_norm_kernel(a_ref, w_ref, h_ref, g_ref, *refs, nk, n_gains, emit_h):
    outs, acc_ref = refs[:-1], refs[-1]
    k = pl.program_id(1)
    part = jnp.dot(a_ref[...], w_ref[...], preferred_element_type=F32)

    @pl.when(k == 0)
    def _():
        acc_ref[...] = part

    @pl.when(k > 0)
    def _():
        acc_ref[...] = acc_ref[...] + part

    @pl.when(k == nk - 1)
    def _():
        hn = acc_ref[...] + h_ref[...]
        if emit_h:
            outs[0][...] = hn
        if n_gains:
            z = hn * lax.rsqrt(jnp.mean(hn * hn, axis=-1, keepdims=True) + EPS)
            for t in range(n_gains):
                o = outs[t + int(emit_h)]
                o[...] = (z * g_ref[t:t + 1, :]).astype(o.dtype)


def _mm_res_norm(a, w, h, gains, *, emit_h=True, norm_dtype=BF16):
    m, kdim = a.shape
    d = w.shape[1]
    n_gains = gains.shape[0]
    bm = _largest_tile(m, 256, SUBLANES)
    bk = _largest_tile(kdim, 512, LANES)
    nk = kdim // bk
    norm_bytes = jnp.dtype(norm_dtype).itemsize
    vmem = (2 * bm * bk * 2 + 2 * bk * d * 2 + 2 * bm * d * 4 + bm * d * 4 * 2
            + 2 * bm * d * 4 * int(emit_h) + 2 * bm * d * norm_bytes * n_gains)
    row = pl.BlockSpec((bm, d), lambda i, k: (i, 0))
    out_shape = ([jax.ShapeDtypeStruct((m, d), F32)] if emit_h else []) + \
        [jax.ShapeDtypeStruct((m, d), norm_dtype)] * n_gains
    g_in = gains if n_gains else jnp.zeros((1, d), F32)
    return pl.pallas_call(
        functools.partial(_mm_res_norm_kernel, nk=nk, n_gains=n_gains, emit_h=emit_h),
        grid=(m // bm, nk),
        in_specs=[
            pl.BlockSpec((bm, bk), lambda i, k: (i, k)),
            pl.BlockSpec((bk, d), lambda i, k: (k, 0)),
            row,
            pl.BlockSpec(g_in.shape, lambda i, k: (0, 0)),
        ],
        out_specs=[row] * len(out_shape),
        out_shape=out_shape,
        scratch_shapes=[pltpu.VMEM((bm, d), F32)],
        compiler_params=_params(vmem, 2),
        name="mm_res_norm",
    )(a, w, h, g_in)


def _mm_kernel(a_ref, w_ref, o_ref):
    o_ref[...] = jnp.dot(a_ref[...], w_ref[...], preferred_element_type=F32).astype(o_ref.dtype)


def _mm(a, w):
    m, kdim = a.shape
    n = w.shape[1]
    bm = _largest_tile(m, 1024, SUBLANES)
    bn = _largest_tile(n, 1024, LANES)
    vmem = 2 * bm * kdim * 2 + 2 * kdim * bn * 2 + 2 * bm * bn * 2 + bm * bn * 4
    return pl.pallas_call(
        _mm_kernel,
        grid=(m // bm, n // bn),
        in_specs=[pl.BlockSpec((bm, kdim), lambda i, j: (i, 0)),
                  pl.BlockSpec((kdim, bn), lambda i, j: (0, j))],
        out_specs=pl.BlockSpec((bm, bn), lambda i, j: (i, j)),
        out_shape=jax.ShapeDtypeStruct((m, n), BF16),
        compiler_params=_params(vmem, 2),
        name="mm",
    )(a, w)


def _fgate_kernel(a_ref, w_ref, b_ref, o_ref):
    x = jnp.dot(a_ref[...], w_ref[...], preferred_element_type=F32) + b_ref[...]
    c = jnp.minimum(x, 0.0) - jnp.log1p(jnp.exp(-jnp.abs(x)))
    s = c.shape[0]
    rows = lax.broadcasted_iota(jnp.int32, c.shape, 0)
    shift = 1
    while shift < s:
        c = c + jnp.where(rows >= shift, pltpu.roll(c, shift, 0), 0.0)
        shift *= 2
    o_ref[...] = c


def _fgate(n_kv, w_f, b_f):
    b, s, d = n_kv.shape
    hp = w_f.shape[1]
    vmem = 2 * s * d * 2 + 2 * d * hp * 2 + 6 * s * hp * 4
    return pl.pallas_call(
        _fgate_kernel,
        grid=(b,),
        in_specs=[pl.BlockSpec((None, s, d), lambda i: (i, 0, 0)),
                  pl.BlockSpec((d, hp), lambda i: (0, 0)),
                  pl.BlockSpec((1, hp), lambda i: (0, 0))],
        out_specs=pl.BlockSpec((None, s, hp), lambda i: (i, 0, 0)),
        out_shape=jax.ShapeDtypeStruct((b, s, hp), F32),
        compiler_params=_params(vmem, 1),
        name="fgate",
    )(n_kv, w_f, b_f)


def _attn_kernel(q_ref, k_ref, v_ref, cq_ref, ck_ref, o_ref, *, tq):
    s = q_ref.shape[0]
    scale = HEAD_DIM ** -0.5
    neg = jnp.finfo(F32).min
    for i in range(s // tq):
        q0, kl = i * tq, (i + 1) * tq
        logits = lax.dot_general(q_ref[q0:kl, :], k_ref[0:kl, :], (((1,), (1,)), ((), ())),
                                 preferred_element_type=F32) * scale
        logits = logits + cq_ref[q0:kl, :] - ck_ref[:, 0:kl]
        row = q0 + lax.broadcasted_iota(jnp.int32, (tq, kl), 0)
        col = lax.broadcasted_iota(jnp.int32, (tq, kl), 1)
        logits = jnp.where(col <= row, logits, neg)
        m = jnp.max(logits, axis=-1, keepdims=True)
        p = jnp.exp(logits - m)
        l = jnp.sum(p, axis=-1, keepdims=True)
        o = jnp.dot(p.astype(BF16), v_ref[0:kl, :], preferred_element_type=F32)
        o_ref[q0:kl, :] = (o / l).astype(o_ref.dtype)


def _attention(q, kv, cum_q, cum_k):
    b, s, d = q.shape
    nh = d // HEAD_DIM
    tq = _largest_tile(s, 256, LANES)
    vmem = 2 * 4 * s * HEAD_DIM * 2 + 2 * s * LANES * 4 + 2 * SUBLANES * s * 4 + 6 * tq * s * 4
    head = lambda i, j: (i, 0, j)
    return pl.pallas_call(
        functools.partial(_attn_kernel, tq=tq),
        grid=(b, nh),
        in_specs=[pl.BlockSpec((None, s, HEAD_DIM), head),
                  pl.BlockSpec((None, s, HEAD_DIM), head),
                  pl.BlockSpec((None, s, HEAD_DIM), lambda i, j: (i, 0, j + nh)),
                  pl.BlockSpec((None, None, s, 1), lambda i, j: (i, j, 0, 0)),
                  pl.BlockSpec((None, None, 1, s), lambda i, j: (i, j, 0, 0))],
        out_specs=pl.BlockSpec((None, s, HEAD_DIM), head),
        out_shape=jax.ShapeDtypeStruct((b, s, d), BF16),
        compiler_params=_params(vmem, 2),
        name="attention",
    )(q, kv, kv, cum_q, cum_k)


def _pad_last(x, n):
    return jnp.pad(x, [(0, 0)] * (x.ndim - 1) + [(0, n - x.shape[-1])])


def _split_pad(x, fp):
    f = x.shape[-1] // 2
    return jnp.concatenate([_pad_last(x[..., :f], fp), _pad_last(x[..., f:], fp)], axis=-1)


def kernel(x, ln_mix, ln_ffn, pool_w, pool_scale, kv_norm, w_kvf, b_f, w_q, w_o, w_up, conv_w,
           conv_b, w_down, final_norm):
    b, s, d = x.shape
    m = b * s
    depth = ln_mix.shape[0]
    n_a = pool_w.shape[0]
    nh = d // HEAD_DIM
    f = w_down.shape[1]
    fp = _round_up(f, 1024)

    w_up_b = _split_pad(w_up, fp).astype(BF16)
    conv_w_p = _split_pad(conv_w, fp)
    conv_b_p = _split_pad(conv_b, fp)[:, None, :]
    w_down_b = jnp.pad(w_down, ((0, 0), (0, fp - f), (0, 0))).astype(BF16)
    pool_w_b = pool_w.astype(BF16)
    w_q_b = w_q.astype(BF16)
    w_o_b = w_o.astype(BF16)
    w_kv_b = w_kvf[:, :2 * d].astype(BF16)
    hp = _round_up(nh, LANES)
    w_f_b = _pad_last(w_kvf[:, 2 * d:], hp).astype(BF16)
    b_f_p = _pad_last(b_f[None, :], hp)

    def ffn(h2d, n2, layer, gains, **kw):
        act = _ffn_up(n2.reshape(b, s, d), w_up_b[layer], conv_w_p[layer], conv_b_p[layer])
        return _mm_res_norm(act.reshape(m, fp), w_down_b[layer], h2d, gains, **kw)

    h = x
    outs = None
    for layer in range(n_a):
        h, n2 = _pool_layer(h, ln_mix[layer][None], ln_ffn[layer][None], pool_scale[layer][None],
                            pool_w_b[layer])
        if layer + 1 < n_a:
            gains = jnp.zeros((0, d), F32)
        else:
            gains = jnp.stack([kv_norm, ln_mix[n_a]])
        outs = ffn(h.reshape(m, d), n2, layer, gains)
        h = outs[0].reshape(b, s, d)

    h2d, n_kv, n_q = outs
    kv = _mm(n_kv, w_kv_b).reshape(b, s, 2 * d)
    cum = _fgate(n_kv.reshape(b, s, d), w_f_b, b_f_p)[:, :, :nh]
    cum = cum.transpose(0, 2, 1)
    cum_q, cum_k = cum[:, :, :, None], cum[:, :, None, :]

    for j in range(depth - n_a):
        layer = n_a + j
        q = _mm(n_q, w_q_b[j]).reshape(b, s, d)
        o = _attention(q, kv, cum_q, cum_k)
        h2d, n2 = _mm_res_norm(o.reshape(m, d), w_o_b[j], h2d, ln_ffn[layer][None])
        if layer + 1 < depth:
            h2d, n_q = ffn(h2d, n2, layer, ln_mix[layer + 1][None])
        else:
            (out,) = ffn(h2d, n2, layer, final_norm[None], emit_h=False, norm_dtype=F32)
    return out.reshape(b, s, d)
```

```python
import functools
import math

import jax
import jax.numpy as jnp
from jax import lax
from jax.experimental import pallas as pl
from jax.experimental.pallas import tpu as pltpu

EPS = 1e-6
POOL_WINDOWS = (2, 4, 8, 16)
POOL_CARRY_ROWS = 16
HEAD_DIM = 128
CONV_WIDTH = 3
LOG2E = math.log2(math.e)
SUBLANES = 8
BF16_ROWS = 16
LANES = 128
V7X_VMEM_BYTES = 64 * 1024 * 1024
VMEM_RESERVE_BYTES = 6 * 1024 * 1024

F32 = jnp.float32
BF16 = jnp.bfloat16


def _round_up(x, m):
    return (x + m - 1) // m * m


def _largest_tile(n, cap, quantum):
    if n <= cap:
        return n
    t = cap - cap % quantum
    while t >= quantum:
        if n % t == 0:
            return t
        t -= quantum
    raise ValueError(f"no tile of {n} is a multiple of {quantum} and <= {cap}")


def _params(vmem_estimate_bytes, n_grid_dims):
    limit = min(int(vmem_estimate_bytes * 5 // 4) + VMEM_RESERVE_BYTES,
                V7X_VMEM_BYTES - VMEM_RESERVE_BYTES)
    return pltpu.CompilerParams(
        dimension_semantics=("arbitrary",) * n_grid_dims,
        vmem_limit_bytes=limit,
    )


def _rms(x, g):
    ms = jnp.mean(x * x, axis=-1, keepdims=True)
    return (x * lax.rsqrt(ms + EPS)) * g


def _pool_kernel(h_ref, gm_ref, gf_ref, sc_ref, w_ref, ho_ref, n2_ref, carry_ref):
    s = pl.program_id(1)
    ts, d = h_ref.shape
    c = d // len(POOL_WINDOWS)

    @pl.when(s == 0)
    def _():
        carry_ref[...] = jnp.zeros_like(carry_ref)

    x = h_ref[...]
    n = _rms(x, gm_ref[...])
    pos = s * ts + lax.broadcasted_iota(jnp.int32, (ts, 1), 0)
    for g, w in enumerate(POOL_WINDOWS):
        cols = slice(g * c, (g + 1) * c)
        acc = jnp.concatenate([carry_ref[:, cols], n[:, cols]], axis=0)
        shift = 1
        while shift < w:
            acc = acc + pltpu.roll(acc, shift, 0)
            shift *= 2
        count = jnp.minimum(pos + 1, w).astype(F32)
        mix = (acc[POOL_CARRY_ROWS:, :] / count - n[:, cols]).astype(BF16)
        y = jnp.dot(mix, w_ref[g], preferred_element_type=F32)
        ho_ref[:, cols] = x[:, cols] + y * sc_ref[:, cols]
    carry_ref[...] = n[ts - POOL_CARRY_ROWS:, :]
    n2_ref[...] = _rms(ho_ref[...], gf_ref[...]).astype(n2_ref.dtype)


def _pool_layer(h, g_mix, g_ffn, scale, w_pool):
    b, s, d = h.shape
    ts = _largest_tile(s, 128, POOL_CARRY_ROWS)
    vmem = 2 * ts * d * (4 + 4 + 2) + 2 * w_pool.size * 2 + 8 * ts * d * 4
    row = pl.BlockSpec((None, ts, d), lambda i, j: (i, j, 0))
    vec = pl.BlockSpec((1, d), lambda i, j: (0, 0))
    return pl.pallas_call(
        _pool_kernel,
        grid=(b, s // ts),
        in_specs=[row, vec, vec, vec, pl.BlockSpec(w_pool.shape, lambda i, j: (0, 0, 0))],
        out_specs=[row, row],
        out_shape=[jax.ShapeDtypeStruct(h.shape, F32), jax.ShapeDtypeStruct(h.shape, BF16)],
        scratch_shapes=[pltpu.VMEM((POOL_CARRY_ROWS, d), F32)],
        compiler_params=_params(vmem, 2),
        name="pool_layer",
    )(h, g_mix, g_ffn, scale, w_pool)


def _causal_conv(u, tail, w_ref, b_ref):
    rows = lax.broadcasted_iota(jnp.int32, tail.shape, 0)

    def lagged(k):
        r = pltpu.roll(u, k, 0)
        head = jnp.where(rows < k, pltpu.roll(tail, k, 0), r[:SUBLANES])
        return jnp.concatenate([head, r[SUBLANES:]], axis=0)

    y = b_ref[...] + w_ref[0:1, :] * u
    for k in range(1, CONV_WIDTH):
        y = y + w_ref[k:k + 1, :] * lagged(k)
    return y


def _ffn_up_kernel(a_ref, wg_ref, wv_ref, cwg_ref, cwv_ref, cbg_ref, cbv_ref, o_ref, *, rc):
    s, bn = o_ref.shape
    tail_g = jnp.zeros((SUBLANES, bn), F32)
    tail_v = jnp.zeros((SUBLANES, bn), F32)
    for r0 in range(0, s, rc):
        a = a_ref[r0:r0 + rc, :]
        ug = jnp.dot(a, wg_ref[...], preferred_element_type=F32)
        uv = jnp.dot(a, wv_ref[...], preferred_element_type=F32)
        yg = _causal_conv(ug, tail_g, cwg_ref, cbg_ref)
        yv = _causal_conv(uv, tail_v, cwv_ref, cbv_ref)
        o_ref[r0:r0 + rc, :] = (yg * jax.nn.sigmoid(yg) * yv).astype(o_ref.dtype)
        tail_g = ug[rc - SUBLANES:, :]
        tail_v = uv[rc - SUBLANES:, :]


def _ffn_up(n2, w_gate, w_val, cw_gate, cw_val, cb_gate, cb_val):
    b, s, d = n2.shape
    fp = w_gate.shape[1]
    bn = _largest_tile(fp, 512, LANES)
    rc = _largest_tile(s, 256, SUBLANES)
    vmem = s * d * 2 + 4 * d * bn * 2 + 2 * s * bn * 2 + 12 * rc * bn * 4
    col = lambda i, j: (0, j)
    w_spec = pl.BlockSpec((d, bn), col)
    cw_spec = pl.BlockSpec((CONV_WIDTH, bn), col)
    cb_spec = pl.BlockSpec((1, bn), col)
    return pl.pallas_call(
        functools.partial(_ffn_up_kernel, rc=rc),
        grid=(b, fp // bn),
        in_specs=[
            pl.BlockSpec((None, s, d), lambda i, j: (i, 0, 0), pipeline_mode=pl.Buffered(1)),
            w_spec, w_spec, cw_spec, cw_spec, cb_spec, cb_spec,
        ],
        out_specs=pl.BlockSpec((None, s, bn), lambda i, j: (i, 0, j)),
        out_shape=jax.ShapeDtypeStruct((b, s, fp), BF16),
        compiler_params=_params(vmem, 2),
        name="ffn_up",
    )(n2, w_gate, w_val, cw_gate, cw_val, cb_gate, cb_val)


def _mm_res_norm_kernel(a_ref, w_ref, h_ref, g_ref, *refs, ni, ne, cn, n_gains, emit_h):
    outs, acc_ref = refs[:-1], refs[-1]
    i, k = pl.program_id(0), pl.program_id(1)
    _, bm, d = acc_ref.shape
    rs = bm // ne

    @pl.when((i == 0) & (k == 0))
    def _():
        acc_ref[...] = jnp.zeros_like(acc_ref)

    def finish_rows(slot):
        r0 = pl.multiple_of(jnp.minimum(k, ne - 1) * rs, rs)
        hn = acc_ref[slot, pl.ds(r0, rs), :] + h_ref[...]
        if emit_h:
            outs[0][...] = hn
        if n_gains:
            z = hn * lax.rsqrt(jnp.mean(hn * hn, axis=-1, keepdims=True) + EPS)
            for t in range(n_gains):
                o = outs[t + int(emit_h)]
                o[...] = (z * g_ref[t:t + 1, :]).astype(o.dtype)

    @pl.when(i < ni)
    def _():
        cur = i % 2
        finish_rows(1 - cur)
        for c in range(0, d, cn):
            part = jnp.dot(a_ref[...], w_ref[:, c:c + cn], preferred_element_type=F32)
            old = acc_ref[cur, :, c:c + cn]
            acc_ref[cur, :, c:c + cn] = jnp.where(k > 0, old, 0.0) + part

    @pl.when(i == ni)
    def _():
        finish_rows((ni - 1) % 2)


def _mm_res_norm(a, w, h, gains, *, emit_h=True, norm_dtype=BF16):
    m, kdim = a.shape
    d = w.shape[1]
    n_gains = gains.shape[0]
    bm = _largest_tile(m, 1024, BF16_ROWS)
    bk = _largest_tile(kdim, 512, LANES)
    cn = _largest_tile(d, 512, LANES)
    ni, nk = m // bm, kdim // bk
    ne = 1
    while ne * 2 <= nk and bm // (ne * 2) >= 64:
        ne *= 2
    rs = bm // ne
    norm_bytes = jnp.dtype(norm_dtype).itemsize
    vmem = (2 * bm * d * 4 + 2 * bm * bk * 2 + 2 * bk * d * 2 + 2 * bm * cn * 4
            + rs * d * (2 * 4 + 2 * 4 * int(emit_h) + 2 * norm_bytes * n_gains + 4 * 4))

    def sub_block(i, k):
        return (jnp.maximum(i - 1, 0) * ne + jnp.where(i == 0, 0, jnp.minimum(k, ne - 1)), 0)

    row = pl.BlockSpec((rs, d), sub_block)
    out_shape = ([jax.ShapeDtypeStruct((m, d), F32)] if emit_h else []) + \
        [jax.ShapeDtypeStruct((m, d), norm_dtype)] * n_gains
    g_in = gains if n_gains else jnp.zeros((1, d), F32)
    return pl.pallas_call(
        functools.partial(_mm_res_norm_kernel, ni=ni, ne=ne, cn=cn, n_gains=n_gains, emit_h=emit_h),
        grid=(ni + 1, nk),
        in_specs=[
            pl.BlockSpec((bm, bk), lambda i, k: (jnp.minimum(i, ni - 1), jnp.where(i < ni, k, nk - 1))),
            pl.BlockSpec((bk, d), lambda i, k: (jnp.where(i < ni, k, nk - 1), 0)),
            row,
            pl.BlockSpec(g_in.shape, lambda i, k: (0, 0)),
        ],
        out_specs=[row] * len(out_shape),
        out_shape=out_shape,
        scratch_shapes=[pltpu.VMEM((2, bm, d), F32)],
        compiler_params=_params(vmem, 2),
        name="mm_res_norm",
    )(a, w, h, g_in)


def _mm_kernel(a_ref, w_ref, o_ref, *, scale):
    acc = jnp.dot(a_ref[...], w_ref[...], preferred_element_type=F32)
    if scale is not None:
        acc = acc * scale
    o_ref[...] = acc.astype(o_ref.dtype)


def _mm(a, w, scale=None):
    m, kdim = a.shape
    n = w.shape[1]
    bm = _largest_tile(m, 1024, BF16_ROWS)
    bn = _largest_tile(n, 1024, LANES)
    vmem = 2 * bm * kdim * 2 + 2 * kdim * bn * 2 + 2 * bm * bn * 2 + bm * bn * 4
    return pl.pallas_call(
        functools.partial(_mm_kernel, scale=scale),
        grid=(m // bm, n // bn),
        in_specs=[pl.BlockSpec((bm, kdim), lambda i, j: (i, 0)),
                  pl.BlockSpec((kdim, bn), lambda i, j: (0, j))],
        out_specs=pl.BlockSpec((bm, bn), lambda i, j: (i, j)),
        out_shape=jax.ShapeDtypeStruct((m, n), BF16),
        compiler_params=_params(vmem, 2),
        name="mm",
    )(a, w)


def _fgate_kernel(a_ref, w_ref, b_ref, o_ref):
    x = jnp.dot(a_ref[...], w_ref[...], preferred_element_type=F32) + b_ref[...]
    c = jnp.minimum(x, 0.0) - jnp.log1p(jnp.exp(-jnp.abs(x)))
    s = c.shape[0]
    rows = lax.broadcasted_iota(jnp.int32, c.shape, 0)
    shift = 1
    while shift < s:
        c = c + jnp.where(rows >= shift, pltpu.roll(c, shift, 0), 0.0)
        shift *= 2
    o_ref[...] = c * LOG2E


def _fgate(n_kv, w_f, b_f):
    b, s, d = n_kv.shape
    hp = w_f.shape[1]
    vmem = 2 * s * d * 2 + 2 * d * hp * 2 + 6 * s * hp * 4
    return pl.pallas_call(
        _fgate_kernel,
        grid=(b,),
        in_specs=[pl.BlockSpec((None, s, d), lambda i: (i, 0, 0)),
                  pl.BlockSpec((d, hp), lambda i: (0, 0)),
                  pl.BlockSpec((1, hp), lambda i: (0, 0))],
        out_specs=pl.BlockSpec((None, s, hp), lambda i: (i, 0, 0)),
        out_shape=jax.ShapeDtypeStruct((b, s, hp), F32),
        compiler_params=_params(vmem, 1),
        name="fgate",
    )(n_kv, w_f, b_f)


def _attn_kernel(q_ref, k_ref, v_ref, cq_ref, ck_ref, o_ref, *, tq):
    s = q_ref.shape[0]
    nt = (((1,), (1,)), ((), ()))
    neg = jnp.finfo(F32).min
    row = lax.broadcasted_iota(jnp.int32, (tq, tq), 0)
    col = lax.broadcasted_iota(jnp.int32, (tq, tq), 1)
    for i in range(s // tq):
        q0, q1 = i * tq, (i + 1) * tq
        q = q_ref[q0:q1, :]
        cq = cq_ref[q0:q1, :]
        diag = lax.dot_general(q, k_ref[q0:q1, :], nt, preferred_element_type=F32)
        diag = jnp.where(col <= row, diag + cq - ck_ref[:, q0:q1], neg)
        m = jnp.max(diag, axis=-1, keepdims=True)
        if i > 0:
            past = lax.dot_general(q, k_ref[0:q0, :], nt, preferred_element_type=F32)
            past = past + cq - ck_ref[:, 0:q0]
            m = jnp.maximum(m, jnp.max(past, axis=-1, keepdims=True))
        p = jnp.exp2(diag - m)
        l = jnp.sum(p, axis=-1, keepdims=True)
        o = jnp.dot(p.astype(BF16), v_ref[q0:q1, :], preferred_element_type=F32)
        if i > 0:
            p = jnp.exp2(past - m)
            l = l + jnp.sum(p, axis=-1, keepdims=True)
            o = o + jnp.dot(p.astype(BF16), v_ref[0:q0, :], preferred_element_type=F32)
        o_ref[q0:q1, :] = (o / l).astype(o_ref.dtype)


def _attention(q, kv, cum_q, cum_k):
    b, s, d = q.shape
    nh = d // HEAD_DIM
    tq = _largest_tile(s, 512, LANES)
    vmem = 2 * 4 * s * HEAD_DIM * 2 + 2 * s * LANES * 4 + 2 * SUBLANES * s * 4 + 6 * tq * s * 4
    head = lambda i, j: (i, 0, j)
    return pl.pallas_call(
        functools.partial(_attn_kernel, tq=tq),
        grid=(b, nh),
        in_specs=[pl.BlockSpec((None, s, HEAD_DIM), head),
                  pl.BlockSpec((None, s, HEAD_DIM), head),
                  pl.BlockSpec((None, s, HEAD_DIM), lambda i, j: (i, 0, j + nh)),
                  pl.BlockSpec((None, None, s, 1), lambda i, j: (i, j, 0, 0)),
                  pl.BlockSpec((None, None, 1, s), lambda i, j: (i, j, 0, 0))],
        out_specs=pl.BlockSpec((None, s, HEAD_DIM), head),
        out_shape=jax.ShapeDtypeStruct((b, s, d), BF16),
        compiler_params=_params(vmem, 2),
        name="attention",
    )(q, kv, kv, cum_q, cum_k)


def _pad_last(x, n):
    return jnp.pad(x, [(0, 0)] * (x.ndim - 1) + [(0, n - x.shape[-1])])


def kernel(x, ln_mix, ln_ffn, pool_w, pool_scale, kv_norm, w_kvf, b_f, w_q, w_o, w_up, conv_w,
           conv_b, w_down, final_norm):
    b, s, d = x.shape
    m = b * s
    depth = ln_mix.shape[0]
    n_a = pool_w.shape[0]
    nh = d // HEAD_DIM
    f = w_down.shape[1]
    fp = _round_up(f, 1024)
    hp = _round_up(nh, LANES)

    def ffn(h2d, n2, layer, gains, **kw):
        halves = lambda p: (_pad_last(p[..., :f], fp), _pad_last(p[..., f:], fp))
        w_gate, w_val = (w.astype(BF16) for w in halves(w_up[layer]))
        cw_gate, cw_val = halves(conv_w[layer])
        cb_gate, cb_val = halves(conv_b[layer][None])
        w_dn = jnp.pad(w_down[layer], ((0, fp - f), (0, 0))).astype(BF16)
        act = _ffn_up(n2.reshape(b, s, d), w_gate, w_val, cw_gate, cw_val, cb_gate, cb_val)
        return _mm_res_norm(act.reshape(m, fp), w_dn, h2d, gains, **kw)

    h = x
    outs = None
    for layer in range(n_a):
        h, n2 = _pool_layer(h, ln_mix[layer][None], ln_ffn[layer][None], pool_scale[layer][None],
                            pool_w[layer].astype(BF16))
        if layer + 1 < n_a:
            gains = jnp.zeros((0, d), F32)
        else:
            gains = jnp.stack([kv_norm, ln_mix[n_a]])
        outs = ffn(h.reshape(m, d), n2, layer, gains)
        h = outs[0].reshape(b, s, d)

    h2d, n_kv, n_q = outs
    kv = _mm(n_kv, w_kvf[:, :2 * d].astype(BF16)).reshape(b, s, 2 * d)
    cum = _fgate(n_kv.reshape(b, s, d), _pad_last(w_kvf[:, 2 * d:], hp).astype(BF16),
                 _pad_last(b_f[None, :], hp))[:, :, :nh]
    cum = cum.transpose(0, 2, 1)
    cum_q, cum_k = cum[:, :, :, None], cum[:, :, None, :]

    for j in range(depth - n_a):
        layer = n_a + j
        q = _mm(n_q, w_q[j].astype(BF16), scale=HEAD_DIM ** -0.5 * LOG2E).reshape(b, s, d)
        o = _attention(q, kv, cum_q, cum_k)
        h2d, n2 = _mm_res_norm(o.reshape(m, d), w_o[j].astype(BF16), h2d, ln_ffn[layer][None])
        if layer + 1 < depth:
            h2d, n_q = ffn(h2d, n2, layer, ln_mix[layer + 1][None])
        else:
            (out,) = ffn(h2d, n2, layer, final_norm[None], emit_h=False, norm_dtype=F32)
    return out.reshape(b, s, d)
```

```python
import functools
import math

import jax
import jax.numpy as jnp
from jax import lax
from jax.experimental import pallas as pl
from jax.experimental.pallas import tpu as pltpu

EPS = 1e-6
POOL_WINDOWS = (2, 4, 8, 16)
POOL_CARRY_ROWS = 16
HEAD_DIM = 128
CONV_WIDTH = 3
LOG2E = math.log2(math.e)
SUBLANES = 8
BF16_ROWS = 16
LANES = 128
W_BLOCK = 256
V7X_VMEM_BYTES = 64 * 1024 * 1024
VMEM_RESERVE_BYTES = 6 * 1024 * 1024

F32 = jnp.float32
BF16 = jnp.bfloat16


def _round_up(x, m):
    return (x + m - 1) // m * m


def _largest_tile(n, cap, quantum):
    if n <= cap:
        return n
    t = cap - cap % quantum
    while t >= quantum:
        if n % t == 0:
            return t
        t -= quantum
    raise ValueError(f"no tile of {n} is a multiple of {quantum} and <= {cap}")


def _params(vmem_estimate_bytes, n_grid_dims):
    limit = min(int(vmem_estimate_bytes * 5 // 4) + VMEM_RESERVE_BYTES,
                V7X_VMEM_BYTES - VMEM_RESERVE_BYTES)
    return pltpu.CompilerParams(
        dimension_semantics=("arbitrary",) * n_grid_dims,
        vmem_limit_bytes=limit,
    )


def _rms(x, g):
    ms = jnp.mean(x * x, axis=-1, keepdims=True)
    return (x * lax.rsqrt(ms + EPS)) * g


def _pool_kernel(h_ref, gm_ref, gf_ref, sc_ref, w_ref, ho_ref, n2_ref, carry_ref):
    s = pl.program_id(1)
    ts, d = h_ref.shape
    c = d // len(POOL_WINDOWS)

    @pl.when(s == 0)
    def _():
        carry_ref[...] = jnp.zeros_like(carry_ref)

    x = h_ref[...]
    n = _rms(x, gm_ref[...])
    pos = s * ts + lax.broadcasted_iota(jnp.int32, (ts, 1), 0)
    for g, w in enumerate(POOL_WINDOWS):
        cols = slice(g * c, (g + 1) * c)
        acc = jnp.concatenate([carry_ref[:, cols], n[:, cols]], axis=0)
        shift = 1
        while shift < w:
            acc = acc + pltpu.roll(acc, shift, 0)
            shift *= 2
        count = jnp.minimum(pos + 1, w).astype(F32)
        mix = (acc[POOL_CARRY_ROWS:, :] / count - n[:, cols]).astype(BF16)
        y = jnp.dot(mix, w_ref[g], preferred_element_type=F32)
        ho_ref[:, cols] = x[:, cols] + y * sc_ref[:, cols]
    carry_ref[...] = n[ts - POOL_CARRY_ROWS:, :]
    n2_ref[...] = _rms(ho_ref[...], gf_ref[...]).astype(n2_ref.dtype)


def _pool_layer(h, g_mix, g_ffn, scale, w_pool):
    b, s, d = h.shape
    ts = _largest_tile(s, 128, POOL_CARRY_ROWS)
    vmem = 2 * ts * d * (4 + 4 + 2) + 2 * w_pool.size * 2 + 8 * ts * d * 4
    row = pl.BlockSpec((None, ts, d), lambda i, j: (i, j, 0))
    vec = pl.BlockSpec((1, d), lambda i, j: (0, 0))
    return pl.pallas_call(
        _pool_kernel,
        grid=(b, s // ts),
        in_specs=[row, vec, vec, vec, pl.BlockSpec(w_pool.shape, lambda i, j: (0, 0, 0))],
        out_specs=[row, row],
        out_shape=[jax.ShapeDtypeStruct(h.shape, F32), jax.ShapeDtypeStruct(h.shape, BF16)],
        scratch_shapes=[pltpu.VMEM((POOL_CARRY_ROWS, d), F32)],
        compiler_params=_params(vmem, 2),
        name="pool_layer",
    )(h, g_mix, g_ffn, scale, w_pool)


def _cast_blocks_kernel(*refs, n_valid):
    j = pl.program_id(1)
    n = len(refs) // 2
    for x_ref, o_ref in zip(refs[:n], refs[n:]):
        o_ref[...] = jnp.where(j < n_valid, x_ref[...], 0.0).astype(o_ref.dtype)


def _cast_blocks(xs, grid, in_specs, out_specs, out_shapes, n_valid):
    block_elems = sum(math.prod(d for d in sp.block_shape if d is not None) for sp in in_specs)
    return pl.pallas_call(
        functools.partial(_cast_blocks_kernel, n_valid=n_valid),
        grid=grid,
        in_specs=in_specs,
        out_specs=out_specs,
        out_shape=[jax.ShapeDtypeStruct(sh, BF16) for sh in out_shapes],
        compiler_params=_params(2 * block_elems * (4 + 2) + block_elems * 4, 2),
        name="cast_blocks",
    )(*xs)


def _prep_up(w_up, fp):
    n_layers, d, f2 = w_up.shape
    nv, nb = f2 // 2 // W_BLOCK, fp // W_BLOCK
    src = lambda half: pl.BlockSpec(
        (None, d, W_BLOCK), lambda l, j: (l, 0, half * nv + jnp.minimum(j, nv - 1)))
    dst = pl.BlockSpec((None, None, d, W_BLOCK), lambda l, j: (l, j, 0, 0))
    return _cast_blocks([w_up, w_up], (n_layers, nb), [src(0), src(1)], [dst, dst],
                        [(n_layers, nb, d, W_BLOCK)] * 2, nv)


def _prep_down(w_down, fp):
    n_layers, f, d = w_down.shape
    nv, nb = f // W_BLOCK, fp // W_BLOCK
    src = pl.BlockSpec((None, W_BLOCK, d), lambda l, j: (l, jnp.minimum(j, nv - 1), 0))
    dst = pl.BlockSpec((None, W_BLOCK, d), lambda l, j: (l, j, 0))
    (out,) = _cast_blocks([w_down], (n_layers, nb), [src], [dst], [(n_layers, fp, d)], nv)
    return out


def _prep_cols(w, n_cols):
    r = w.shape[0]
    tr = _largest_tile(r, 512, BF16_ROWS)
    src = pl.BlockSpec((tr, n_cols), lambda l, j: (j, 0))
    dst = pl.BlockSpec((None, tr, n_cols), lambda l, j: (0, j, 0))
    (out,) = _cast_blocks([w], (1, r // tr), [src], [dst], [(1, r, n_cols)], r // tr)
    return out


def _causal_conv(u, tail, w, b):
    rows = lax.broadcasted_iota(jnp.int32, tail.shape, 0)

    def lagged(k):
        r = pltpu.roll(u, k, 0)
        head = jnp.where(rows < k, pltpu.roll(tail, k, 0), r[:SUBLANES])
        return jnp.concatenate([head, r[SUBLANES:]], axis=0)

    y = b + w[0:1, :] * u
    for k in range(1, CONV_WIDTH):
        y = y + w[k:k + 1, :] * lagged(k)
    return y


def _ffn_up_kernel(a_ref, wg_ref, wv_ref, cwg_ref, cwv_ref, cbg_ref, cbv_ref, o_ref, *, rc):
    s = o_ref.shape[0]
    nb, _, wb = wg_ref.shape
    zeros = jnp.zeros((SUBLANES, wb), F32)
    tails = [(zeros, zeros)] * nb
    for r0 in range(0, s, rc):
        a = a_ref[r0:r0 + rc, :]
        for t in range(nb):
            cols = slice(t * wb, (t + 1) * wb)
            ug = jnp.dot(a, wg_ref[t], preferred_element_type=F32)
            uv = jnp.dot(a, wv_ref[t], preferred_element_type=F32)
            yg = _causal_conv(ug, tails[t][0], cwg_ref[:, cols], cbg_ref[:, cols])
            yv = _causal_conv(uv, tails[t][1], cwv_ref[:, cols], cbv_ref[:, cols])
            o_ref[r0:r0 + rc, cols] = (yg * jax.nn.sigmoid(yg) * yv).astype(o_ref.dtype)
            tails[t] = (ug[rc - SUBLANES:, :], uv[rc - SUBLANES:, :])


def _ffn_up(n2, w_gate, w_val, layer, cw_gate, cw_val, cb_gate, cb_val):
    b, s, d = n2.shape
    fp = cw_gate.shape[1]
    nb = 2
    bn = nb * W_BLOCK
    rc = _largest_tile(s, 256, SUBLANES)
    vmem = s * d * 2 + 4 * d * bn * 2 + 2 * s * bn * 2 + 12 * rc * bn * 4
    col = lambda i, j: (0, j)
    w_spec = pl.BlockSpec((None, nb, d, W_BLOCK), lambda i, j: (layer, j, 0, 0))
    cw_spec = pl.BlockSpec((CONV_WIDTH, bn), col)
    cb_spec = pl.BlockSpec((1, bn), col)
    return pl.pallas_call(
        functools.partial(_ffn_up_kernel, rc=rc),
        grid=(b, fp // bn),
        in_specs=[
            pl.BlockSpec((None, s, d), lambda i, j: (i, 0, 0), pipeline_mode=pl.Buffered(1)),
            w_spec, w_spec, cw_spec, cw_spec, cb_spec, cb_spec,
        ],
        out_specs=pl.BlockSpec((None, s, bn), lambda i, j: (i, 0, j)),
        out_shape=jax.ShapeDtypeStruct((b, s, fp), BF16),
        compiler_params=_params(vmem, 2),
        name="ffn_up",
    )(n2, w_gate, w_val, cw_gate, cw_val, cb_gate, cb_val)


def _mm_res_norm_kernel(a_ref, w_ref, h_ref, g_ref, *refs, ni, ne, cn, n_gains, emit_h):
    outs, acc_ref = refs[:-1], refs[-1]
    i, k = pl.program_id(0), pl.program_id(1)
    _, bm, d = acc_ref.shape
    rs = bm // ne

    @pl.when((i == 0) & (k == 0))
    def _():
        acc_ref[...] = jnp.zeros_like(acc_ref)

    def finish_rows(slot):
        r0 = pl.multiple_of(jnp.minimum(k, ne - 1) * rs, rs)
        hn = acc_ref[slot, pl.ds(r0, rs), :] + h_ref[...]
        if emit_h:
            outs[0][...] = hn
        if n_gains:
            z = hn * lax.rsqrt(jnp.mean(hn * hn, axis=-1, keepdims=True) + EPS)
            for t in range(n_gains):
                o = outs[t + int(emit_h)]
                o[...] = (z * g_ref[t:t + 1, :]).astype(o.dtype)

    @pl.when(i < ni)
    def _():
        cur = i % 2
        finish_rows(1 - cur)
        for c in range(0, d, cn):
            part = jnp.dot(a_ref[...], w_ref[:, c:c + cn], preferred_element_type=F32)
            old = acc_ref[cur, :, c:c + cn]
            acc_ref[cur, :, c:c + cn] = jnp.where(k > 0, old, 0.0) + part

    @pl.when(i == ni)
    def _():
        finish_rows((ni - 1) % 2)


def _mm_res_norm(a, w, layer, h, gains, *, emit_h=True, norm_dtype=BF16):
    m, kdim = a.shape
    d = w.shape[2]
    n_gains = gains.shape[0]
    bm = _largest_tile(m, 1024, BF16_ROWS)
    bk = _largest_tile(kdim, 512, LANES)
    cn = _largest_tile(d, 512, LANES)
    ni, nk = m // bm, kdim // bk
    ne = 1
    while ne * 2 <= nk and bm // (ne * 2) >= 64:
        ne *= 2
    rs = bm // ne
    norm_bytes = jnp.dtype(norm_dtype).itemsize
    vmem = (2 * bm * d * 4 + 2 * bm * bk * 2 + 2 * bk * d * 2 + 2 * bm * cn * 4
            + rs * d * (2 * 4 + 2 * 4 * int(emit_h) + 2 * norm_bytes * n_gains + 4 * 4))

    def sub_block(i, k):
        return (jnp.maximum(i - 1, 0) * ne + jnp.where(i == 0, 0, jnp.minimum(k, ne - 1)), 0)

    row = pl.BlockSpec((rs, d), sub_block)
    out_shape = ([jax.ShapeDtypeStruct((m, d), F32)] if emit_h else []) + \
        [jax.ShapeDtypeStruct((m, d), norm_dtype)] * n_gains
    g_in = gains if n_gains else jnp.zeros((1, d), F32)
    return pl.pallas_call(
        functools.partial(_mm_res_norm_kernel, ni=ni, ne=ne, cn=cn, n_gains=n_gains, emit_h=emit_h),
        grid=(ni + 1, nk),
        in_specs=[
            pl.BlockSpec((bm, bk), lambda i, k: (jnp.minimum(i, ni - 1), jnp.where(i < ni, k, nk - 1))),
            pl.BlockSpec((None, bk, d), lambda i, k: (layer, jnp.where(i < ni, k, nk - 1), 0)),
            row,
            pl.BlockSpec(g_in.shape, lambda i, k: (0, 0)),
        ],
        out_specs=[row] * len(out_shape),
        out_shape=out_shape,
        scratch_shapes=[pltpu.VMEM((2, bm, d), F32)],
        compiler_params=_params(vmem, 2),
        name="mm_res_norm",
    )(a, w, h, g_in)


def _mm_kernel(a_ref, w_ref, o_ref, *, scale):
    acc = jnp.dot(a_ref[...], w_ref[...], preferred_element_type=F32)
    if scale is not None:
        acc = acc * scale
    o_ref[...] = acc.astype(o_ref.dtype)


def _mm(a, w, layer, scale=None):
    m, kdim = a.shape
    n = w.shape[2]
    bm = _largest_tile(m, 1024, BF16_ROWS)
    bn = _largest_tile(n, 1024, LANES)
    vmem = 2 * bm * kdim * 2 + 2 * kdim * bn * 2 + 2 * bm * bn * 2 + bm * bn * 4
    return pl.pallas_call(
        functools.partial(_mm_kernel, scale=scale),
        grid=(m // bm, n // bn),
        in_specs=[pl.BlockSpec((bm, kdim), lambda i, j: (i, 0)),
                  pl.BlockSpec((None, kdim, bn), lambda i, j: (layer, 0, j))],
        out_specs=pl.BlockSpec((bm, bn), lambda i, j: (i, j)),
        out_shape=jax.ShapeDtypeStruct((m, n), BF16),
        compiler_params=_params(vmem, 2),
        name="mm",
    )(a, w)


def _fgate_kernel(a_ref, w_ref, b_ref, o_ref):
    x = jnp.dot(a_ref[...], w_ref[...], preferred_element_type=F32) + b_ref[...]
    c = jnp.minimum(x, 0.0) - jnp.log1p(jnp.exp(-jnp.abs(x)))
    s = c.shape[0]
    rows = lax.broadcasted_iota(jnp.int32, c.shape, 0)
    shift = 1
    while shift < s:
        c = c + jnp.where(rows >= shift, pltpu.roll(c, shift, 0), 0.0)
        shift *= 2
    o_ref[...] = c * LOG2E


def _fgate(n_kv, w_f, b_f):
    b, s, d = n_kv.shape
    hp = w_f.shape[1]
    vmem = 2 * s * d * 2 + 2 * d * hp * 2 + 6 * s * hp * 4
    return pl.pallas_call(
        _fgate_kernel,
        grid=(b,),
        in_specs=[pl.BlockSpec((None, s, d), lambda i: (i, 0, 0)),
                  pl.BlockSpec((d, hp), lambda i: (0, 0)),
                  pl.BlockSpec((1, hp), lambda i: (0, 0))],
        out_specs=pl.BlockSpec((None, s, hp), lambda i: (i, 0, 0)),
        out_shape=jax.ShapeDtypeStruct((b, s, hp), F32),
        compiler_params=_params(vmem, 1),
        name="fgate",
    )(n_kv, w_f, b_f)


def _attn_kernel(q_ref, k_ref, v_ref, cum_ref, ck_ref, o_ref, *, tq):
    s = q_ref.shape[0]
    head_lane = lax.broadcasted_iota(jnp.int32, (tq, cum_ref.shape[1]), 1) == pl.program_id(1)
    nt = (((1,), (1,)), ((), ()))
    neg = jnp.finfo(F32).min
    row = lax.broadcasted_iota(jnp.int32, (tq, tq), 0)
    col = lax.broadcasted_iota(jnp.int32, (tq, tq), 1)
    for i in range(s // tq):
        q0, q1 = i * tq, (i + 1) * tq
        q = q_ref[q0:q1, :]
        cq = jnp.sum(jnp.where(head_lane, cum_ref[q0:q1, :], 0.0), axis=-1, keepdims=True)
        diag = lax.dot_general(q, k_ref[q0:q1, :], nt, preferred_element_type=F32)
        diag = jnp.where(col <= row, diag + cq - ck_ref[:, q0:q1], neg)
        m = jnp.max(diag, axis=-1, keepdims=True)
        if i > 0:
            past = lax.dot_general(q, k_ref[0:q0, :], nt, preferred_element_type=F32)
            past = past + cq - ck_ref[:, 0:q0]
            m = jnp.maximum(m, jnp.max(past, axis=-1, keepdims=True))
        p = jnp.exp2(diag - m)
        l = jnp.sum(p, axis=-1, keepdims=True)
        o = jnp.dot(p.astype(BF16), v_ref[q0:q1, :], preferred_element_type=F32)
        if i > 0:
            p = jnp.exp2(past - m)
            l = l + jnp.sum(p, axis=-1, keepdims=True)
            o = o + jnp.dot(p.astype(BF16), v_ref[0:q0, :], preferred_element_type=F32)
        o_ref[q0:q1, :] = (o / l).astype(o_ref.dtype)


def _attention(q, kv, cum, cum_k):
    b, s, d = q.shape
    nh = d // HEAD_DIM
    hp = cum.shape[2]
    tq = _largest_tile(s, 512, LANES)
    vmem = 2 * 4 * s * HEAD_DIM * 2 + 2 * s * hp * 4 + 2 * SUBLANES * s * 4 + 6 * tq * s * 4
    head = lambda i, j: (i, 0, j)
    return pl.pallas_call(
        functools.partial(_attn_kernel, tq=tq),
        grid=(b, nh),
        in_specs=[pl.BlockSpec((None, s, HEAD_DIM), head),
                  pl.BlockSpec((None, s, HEAD_DIM), head),
                  pl.BlockSpec((None, s, HEAD_DIM), lambda i, j: (i, 0, j + nh)),
                  pl.BlockSpec((None, s, hp), lambda i, j: (i, 0, 0)),
                  pl.BlockSpec((None, None, 1, s), lambda i, j: (i, j, 0, 0))],
        out_specs=pl.BlockSpec((None, s, HEAD_DIM), head),
        out_shape=jax.ShapeDtypeStruct((b, s, d), BF16),
        compiler_params=_params(vmem, 2),
        name="attention",
    )(q, kv, kv, cum, cum_k)


def _pad_last(x, n):
    return jnp.pad(x, [(0, 0)] * (x.ndim - 1) + [(0, n - x.shape[-1])])


def kernel(x, ln_mix, ln_ffn, pool_w, pool_scale, kv_norm, w_kvf, b_f, w_q, w_o, w_up, conv_w,
           conv_b, w_down, final_norm):
    b, s, d = x.shape
    m = b * s
    depth = ln_mix.shape[0]
    n_a = pool_w.shape[0]
    nh = d // HEAD_DIM
    f = w_down.shape[1]
    fp = _round_up(f, 1024)
    hp = _round_up(nh, LANES)

    w_gate, w_val = _prep_up(w_up, fp)
    w_dn = _prep_down(w_down, fp)
    w_kv = _prep_cols(w_kvf, 2 * d)
    w_q_b, w_o_b, pool_w_b = w_q.astype(BF16), w_o.astype(BF16), pool_w.astype(BF16)
    halves = lambda p: (_pad_last(p[..., :f], fp), _pad_last(p[..., f:], fp))

    def ffn(h2d, n2, layer, gains, **kw):
        cw_gate, cw_val = halves(conv_w[layer])
        cb_gate, cb_val = halves(conv_b[layer][None])
        act = _ffn_up(n2.reshape(b, s, d), w_gate, w_val, layer, cw_gate, cw_val, cb_gate, cb_val)
        return _mm_res_norm(act.reshape(m, fp), w_dn, layer, h2d, gains, **kw)

    h = x
    outs = None
    for layer in range(n_a):
        h, n2 = _pool_layer(h, ln_mix[layer][None], ln_ffn[layer][None], pool_scale[layer][None],
                            pool_w_b[layer])
        if layer + 1 < n_a:
            gains = jnp.zeros((0, d), F32)
        else:
            gains = jnp.stack([kv_norm, ln_mix[n_a]])
        outs = ffn(h.reshape(m, d), n2, layer, gains)
        h = outs[0].reshape(b, s, d)

    h2d, n_kv, n_q = outs
    kv = _mm(n_kv, w_kv, 0).reshape(b, s, 2 * d)
    cum = _fgate(n_kv.reshape(b, s, d), _pad_last(w_kvf[:, 2 * d:], hp).astype(BF16),
                 _pad_last(b_f[None, :], hp))
    cum_k = cum[:, :, :nh].transpose(0, 2, 1)[:, :, None, :]

    for j in range(depth - n_a):
        layer = n_a + j
        q = _mm(n_q, w_q_b, j, scale=HEAD_DIM ** -0.5 * LOG2E).reshape(b, s, d)
        o = _attention(q, kv, cum, cum_k)
        h2d, n2 = _mm_res_norm(o.reshape(m, d), w_o_b, j, h2d, ln_ffn[layer][None])
        if layer + 1 < depth:
            h2d, n_q = ffn(h2d, n2, layer, ln_mix[layer + 1][None])
        else:
            (out,) = ffn(h2d, n2, layer, final_norm[None], emit_h=False, norm_dtype=F32)
    return out.reshape(b, s, d)
```

```python
import functools
import math

import jax
import jax.numpy as jnp
from jax import lax
from jax.experimental import pallas as pl
from jax.experimental.pallas import tpu as pltpu

EPS = 1e-6
POOL_WINDOWS = (2, 4, 8, 16)
POOL_CARRY_ROWS = 16
HEAD_DIM = 128
CONV_WIDTH = 3
LOG2E = math.log2(math.e)
SUBLANES = 8
BF16_ROWS = 16
LANES = 128
W_BLOCK = 256
V7X_VMEM_BYTES = 64 * 1024 * 1024
VMEM_RESERVE_BYTES = 6 * 1024 * 1024

F32 = jnp.float32
BF16 = jnp.bfloat16


def _round_up(x, m):
    return (x + m - 1) // m * m


def _largest_tile(n, cap, quantum):
    if n <= cap:
        return n
    t = cap - cap % quantum
    while t >= quantum:
        if n % t == 0:
            return t
        t -= quantum
    raise ValueError(f"no tile of {n} is a multiple of {quantum} and <= {cap}")


def _params(vmem_estimate_bytes, n_grid_dims):
    limit = min(int(vmem_estimate_bytes * 5 // 4) + VMEM_RESERVE_BYTES,
                V7X_VMEM_BYTES - VMEM_RESERVE_BYTES)
    return pltpu.CompilerParams(
        dimension_semantics=("arbitrary",) * n_grid_dims,
        vmem_limit_bytes=limit,
    )


def _rms(x, g):
    ms = jnp.mean(x * x, axis=-1, keepdims=True)
    return (x * lax.rsqrt(ms + EPS)) * g


def _pool_kernel(h_ref, gm_ref, gf_ref, sc_ref, w_ref, ho_ref, n2_ref, carry_ref):
    s = pl.program_id(1)
    ts, d = h_ref.shape
    c = d // len(POOL_WINDOWS)

    @pl.when(s == 0)
    def _():
        carry_ref[...] = jnp.zeros_like(carry_ref)

    x = h_ref[...]
    n = _rms(x, gm_ref[...])
    pos = s * ts + lax.broadcasted_iota(jnp.int32, (ts, 1), 0)
    for g, w in enumerate(POOL_WINDOWS):
        cols = slice(g * c, (g + 1) * c)
        acc = jnp.concatenate([carry_ref[:, cols], n[:, cols]], axis=0)
        shift = 1
        while shift < w:
            acc = acc + pltpu.roll(acc, shift, 0)
            shift *= 2
        count = jnp.minimum(pos + 1, w).astype(F32)
        mix = (acc[POOL_CARRY_ROWS:, :] / count - n[:, cols]).astype(BF16)
        y = jnp.dot(mix, w_ref[g], preferred_element_type=F32)
        ho_ref[:, cols] = x[:, cols] + y * sc_ref[:, cols]
    carry_ref[...] = n[ts - POOL_CARRY_ROWS:, :]
    n2_ref[...] = _rms(ho_ref[...], gf_ref[...]).astype(n2_ref.dtype)


def _pool_layer(h, g_mix, g_ffn, scale, w_pool):
    b, s, d = h.shape
    ts = _largest_tile(s, 256, POOL_CARRY_ROWS)
    vmem = 2 * ts * d * (4 + 4 + 2) + 2 * w_pool.size * 2 + 8 * ts * d * 4
    row = pl.BlockSpec((None, ts, d), lambda i, j: (i, j, 0))
    vec = pl.BlockSpec((1, d), lambda i, j: (0, 0))
    return pl.pallas_call(
        _pool_kernel,
        grid=(b, s // ts),
        in_specs=[row, vec, vec, vec, pl.BlockSpec(w_pool.shape, lambda i, j: (0, 0, 0))],
        out_specs=[row, row],
        out_shape=[jax.ShapeDtypeStruct(h.shape, F32), jax.ShapeDtypeStruct(h.shape, BF16)],
        scratch_shapes=[pltpu.VMEM((POOL_CARRY_ROWS, d), F32)],
        compiler_params=_params(vmem, 2),
        name="pool_layer",
    )(h, g_mix, g_ffn, scale, w_pool)


def _cast_blocks_kernel(*refs, n_valid):
    j = pl.program_id(1)
    n = len(refs) // 2
    for x_ref, o_ref in zip(refs[:n], refs[n:]):
        o_ref[...] = jnp.where(j < n_valid, x_ref[...], 0.0).astype(o_ref.dtype)


def _cast_blocks(xs, grid, in_specs, out_specs, out_shapes, n_valid):
    block_elems = sum(math.prod(d for d in sp.block_shape if d is not None) for sp in in_specs)
    return pl.pallas_call(
        functools.partial(_cast_blocks_kernel, n_valid=n_valid),
        grid=grid,
        in_specs=in_specs,
        out_specs=out_specs,
        out_shape=[jax.ShapeDtypeStruct(sh, BF16) for sh in out_shapes],
        compiler_params=_params(2 * block_elems * (4 + 2) + block_elems * 4, 2),
        name="cast_blocks",
    )(*xs)


def _prep_up(w_up, fp):
    n_layers, d, f2 = w_up.shape
    nv, nb = f2 // 2 // W_BLOCK, fp // W_BLOCK
    src = lambda half: pl.BlockSpec(
        (None, d, W_BLOCK), lambda l, j: (l, 0, half * nv + jnp.minimum(j, nv - 1)))
    dst = pl.BlockSpec((None, None, d, W_BLOCK), lambda l, j: (l, j, 0, 0))
    return _cast_blocks([w_up, w_up], (n_layers, nb), [src(0), src(1)], [dst, dst],
                        [(n_layers, nb, d, W_BLOCK)] * 2, nv)


def _prep_down(w_down, fp):
    n_layers, f, d = w_down.shape
    nv, nb = f // W_BLOCK, fp // W_BLOCK
    src = pl.BlockSpec((None, W_BLOCK, d), lambda l, j: (l, jnp.minimum(j, nv - 1), 0))
    dst = pl.BlockSpec((None, W_BLOCK, d), lambda l, j: (l, j, 0))
    (out,) = _cast_blocks([w_down], (n_layers, nb), [src], [dst], [(n_layers, fp, d)], nv)
    return out


def _prep_cols(w, n_cols):
    r = w.shape[0]
    tr = _largest_tile(r, 512, BF16_ROWS)
    src = pl.BlockSpec((tr, n_cols), lambda l, j: (j, 0))
    dst = pl.BlockSpec((None, tr, n_cols), lambda l, j: (0, j, 0))
    (out,) = _cast_blocks([w], (1, r // tr), [src], [dst], [(1, r, n_cols)], r // tr)
    return out


def _causal_conv(u, tail, w, b):
    rows = lax.broadcasted_iota(jnp.int32, tail.shape, 0)

    def lagged(k):
        r = pltpu.roll(u, k, 0)
        head = jnp.where(rows < k, pltpu.roll(tail, k, 0), r[:SUBLANES])
        return jnp.concatenate([head, r[SUBLANES:]], axis=0)

    y = b + w[0:1, :] * u
    for k in range(1, CONV_WIDTH):
        y = y + w[k:k + 1, :] * lagged(k)
    return y


def _ffn_up_kernel(a_ref, wg_ref, wv_ref, cwg_ref, cwv_ref, cbg_ref, cbv_ref, o_ref, *, rc):
    s = o_ref.shape[0]
    nb, _, wb = wg_ref.shape
    zeros = jnp.zeros((SUBLANES, wb), F32)
    tails = [(zeros, zeros)] * nb
    for r0 in range(0, s, rc):
        a = a_ref[r0:r0 + rc, :]
        for t in range(nb):
            cols = slice(t * wb, (t + 1) * wb)
            ug = jnp.dot(a, wg_ref[t], preferred_element_type=F32)
            uv = jnp.dot(a, wv_ref[t], preferred_element_type=F32)
            yg = _causal_conv(ug, tails[t][0], cwg_ref[:, cols], cbg_ref[:, cols])
            yv = _causal_conv(uv, tails[t][1], cwv_ref[:, cols], cbv_ref[:, cols])
            o_ref[r0:r0 + rc, cols] = (yg * jax.nn.sigmoid(yg) * yv).astype(o_ref.dtype)
            tails[t] = (ug[rc - SUBLANES:, :], uv[rc - SUBLANES:, :])


def _ffn_up(n2, w_gate, w_val, layer, cw_gate, cw_val, cb_gate, cb_val):
    b, s, d = n2.shape
    fp = cw_gate.shape[1]
    nb = 2
    bn = nb * W_BLOCK
    rc = _largest_tile(s, 256, SUBLANES)
    vmem = 2 * s * d * 2 + 4 * d * bn * 2 + 2 * s * bn * 2 + 4 * rc * bn * 4
    col = lambda i, j: (0, j)
    w_spec = pl.BlockSpec((None, nb, d, W_BLOCK), lambda i, j: (layer, j, 0, 0))
    cw_spec = pl.BlockSpec((CONV_WIDTH, bn), col)
    cb_spec = pl.BlockSpec((1, bn), col)
    return pl.pallas_call(
        functools.partial(_ffn_up_kernel, rc=rc),
        grid=(b, fp // bn),
        in_specs=[
            pl.BlockSpec((None, s, d), lambda i, j: (i, 0, 0)),
            w_spec, w_spec, cw_spec, cw_spec, cb_spec, cb_spec,
        ],
        out_specs=pl.BlockSpec((None, s, bn), lambda i, j: (i, 0, j)),
        out_shape=jax.ShapeDtypeStruct((b, s, fp), BF16),
        compiler_params=_params(vmem, 2),
        name="ffn_up",
    )(n2, w_gate, w_val, cw_gate, cw_val, cb_gate, cb_val)


def _mm_res_norm_kernel(a_ref, w_ref, h_ref, g_ref, *refs, ni, ne, cn, n_gains, emit_h):
    outs, acc_ref = refs[:-1], refs[-1]
    i, k = pl.program_id(0), pl.program_id(1)
    _, bm, d = acc_ref.shape
    rs = bm // ne

    @pl.when((i == 0) & (k == 0))
    def _():
        acc_ref[...] = jnp.zeros_like(acc_ref)

    def finish_rows(slot):
        r0 = pl.multiple_of(jnp.minimum(k, ne - 1) * rs, rs)
        hn = acc_ref[slot, pl.ds(r0, rs), :] + h_ref[...]
        if emit_h:
            outs[0][...] = hn
        if n_gains:
            z = hn * lax.rsqrt(jnp.mean(hn * hn, axis=-1, keepdims=True) + EPS)
            for t in range(n_gains):
                o = outs[t + int(emit_h)]
                o[...] = (z * g_ref[t:t + 1, :]).astype(o.dtype)

    @pl.when(i < ni)
    def _():
        cur = i % 2
        finish_rows(1 - cur)
        for c in range(0, d, cn):
            part = jnp.dot(a_ref[...], w_ref[:, c:c + cn], preferred_element_type=F32)
            old = acc_ref[cur, :, c:c + cn]
            acc_ref[cur, :, c:c + cn] = jnp.where(k > 0, old, 0.0) + part

    @pl.when(i == ni)
    def _():
        finish_rows((ni - 1) % 2)


def _mm_res_norm(a, w, layer, h, gains, *, emit_h=True, norm_dtype=BF16):
    m, kdim = a.shape
    d = w.shape[2]
    n_gains = gains.shape[0]
    bm = _largest_tile(m, 1024, BF16_ROWS)
    bk = _largest_tile(kdim, 512, LANES)
    cn = _largest_tile(d, 512, LANES)
    ni, nk = m // bm, kdim // bk
    ne = 1
    while ne * 2 <= nk and bm // (ne * 2) >= 64:
        ne *= 2
    rs = bm // ne
    norm_bytes = jnp.dtype(norm_dtype).itemsize
    vmem = (2 * bm * d * 4 + 2 * bm * bk * 2 + 2 * bk * d * 2 + 2 * bm * cn * 4
            + rs * d * (2 * 4 + 2 * 4 * int(emit_h) + 2 * norm_bytes * n_gains + 4 * 4))

    def sub_block(i, k):
        return (jnp.maximum(i - 1, 0) * ne + jnp.where(i == 0, 0, jnp.minimum(k, ne - 1)), 0)

    row = pl.BlockSpec((rs, d), sub_block)
    out_shape = ([jax.ShapeDtypeStruct((m, d), F32)] if emit_h else []) + \
        [jax.ShapeDtypeStruct((m, d), norm_dtype)] * n_gains
    g_in = gains if n_gains else jnp.zeros((1, d), F32)
    return pl.pallas_call(
        functools.partial(_mm_res_norm_kernel, ni=ni, ne=ne, cn=cn, n_gains=n_gains, emit_h=emit_h),
        grid=(ni + 1, nk),
        in_specs=[
            pl.BlockSpec((bm, bk), lambda i, k: (jnp.minimum(i, ni - 1), jnp.where(i < ni, k, nk - 1))),
            pl.BlockSpec((None, bk, d), lambda i, k: (layer, jnp.where(i < ni, k, nk - 1), 0)),
            row,
            pl.BlockSpec(g_in.shape, lambda i, k: (0, 0)),
        ],
        out_specs=[row] * len(out_shape),
        out_shape=out_shape,
        scratch_shapes=[pltpu.VMEM((2, bm, d), F32)],
        compiler_params=_params(vmem, 2),
        name="mm_res_norm",
    )(a, w, h, g_in)


def _mm_kernel(a_ref, w_ref, o_ref, *, scale):
    acc = jnp.dot(a_ref[...], w_ref[...], preferred_element_type=F32)
    if scale is not None:
        acc = acc * scale
    o_ref[...] = acc.astype(o_ref.dtype)


def _mm(a, w, layer, scale=None):
    m, kdim = a.shape
    n = w.shape[2]
    bm = _largest_tile(m, 1024, BF16_ROWS)
    bn = _largest_tile(n, 1024, LANES)
    vmem = 2 * bm * kdim * 2 + 2 * kdim * bn * 2 + 2 * bm * bn * 2 + bm * bn * 4
    return pl.pallas_call(
        functools.partial(_mm_kernel, scale=scale),
        grid=(m // bm, n // bn),
        in_specs=[pl.BlockSpec((bm, kdim), lambda i, j: (i, 0)),
                  pl.BlockSpec((None, kdim, bn), lambda i, j: (layer, 0, j))],
        out_specs=pl.BlockSpec((bm, bn), lambda i, j: (i, j)),
        out_shape=jax.ShapeDtypeStruct((m, n), BF16),
        compiler_params=_params(vmem, 2),
        name="mm",
    )(a, w)


def _fgate_kernel(a_ref, w_ref, b_ref, o_ref):
    x = jnp.dot(a_ref[...], w_ref[...], preferred_element_type=F32) + b_ref[...]
    c = jnp.minimum(x, 0.0) - jnp.log1p(jnp.exp(-jnp.abs(x)))
    s = c.shape[0]
    rows = lax.broadcasted_iota(jnp.int32, c.shape, 0)
    shift = 1
    while shift < s:
        c = c + jnp.where(rows >= shift, pltpu.roll(c, shift, 0), 0.0)
        shift *= 2
    o_ref[...] = c * LOG2E


def _fgate(n_kv, w_f, b_f):
    b, s, d = n_kv.shape
    hp = w_f.shape[1]
    vmem = 2 * s * d * 2 + 2 * d * hp * 2 + 6 * s * hp * 4
    return pl.pallas_call(
        _fgate_kernel,
        grid=(b,),
        in_specs=[pl.BlockSpec((None, s, d), lambda i: (i, 0, 0)),
                  pl.BlockSpec((d, hp), lambda i: (0, 0)),
                  pl.BlockSpec((1, hp), lambda i: (0, 0))],
        out_specs=pl.BlockSpec((None, s, hp), lambda i: (i, 0, 0)),
        out_shape=jax.ShapeDtypeStruct((b, s, hp), F32),
        compiler_params=_params(vmem, 1),
        name="fgate",
    )(n_kv, w_f, b_f)


def _attn_kernel(q_ref, k_ref, v_ref, cum_ref, ck_ref, o_ref, *, tq):
    s = q_ref.shape[0]
    head_lane = lax.broadcasted_iota(jnp.int32, (tq, cum_ref.shape[1]), 1) == pl.program_id(1)
    nt = (((1,), (1,)), ((), ()))
    neg = jnp.finfo(F32).min
    row = lax.broadcasted_iota(jnp.int32, (tq, tq), 0)
    col = lax.broadcasted_iota(jnp.int32, (tq, tq), 1)
    for i in reversed(range(s // tq)):
        q0, q1 = i * tq, (i + 1) * tq
        q = q_ref[q0:q1, :]
        cq = jnp.sum(jnp.where(head_lane, cum_ref[q0:q1, :], 0.0), axis=-1, keepdims=True)
        diag = lax.dot_general(q, k_ref[q0:q1, :], nt, preferred_element_type=F32)
        diag = jnp.where(col <= row, diag + cq - ck_ref[:, q0:q1], neg)
        m = jnp.max(diag, axis=-1, keepdims=True)
        if i > 0:
            past = lax.dot_general(q, k_ref[0:q0, :], nt, preferred_element_type=F32)
            past = past + cq - ck_ref[:, 0:q0]
            m = jnp.maximum(m, jnp.max(past, axis=-1, keepdims=True))
        p = jnp.exp2(diag - m)
        l = jnp.sum(p, axis=-1, keepdims=True)
        o = jnp.dot(p.astype(BF16), v_ref[q0:q1, :], preferred_element_type=F32)
        if i > 0:
            p = jnp.exp2(past - m)
            l = l + jnp.sum(p, axis=-1, keepdims=True)
            o = o + jnp.dot(p.astype(BF16), v_ref[0:q0, :], preferred_element_type=F32)
        o_ref[q0:q1, :] = (o / l).astype(o_ref.dtype)


def _attention(q, kv, cum, cum_k):
    b, s, d = q.shape
    nh = d // HEAD_DIM
    hp = cum.shape[2]
    tq = _largest_tile(s, 512, LANES)
    vmem = 2 * 4 * s * HEAD_DIM * 2 + 2 * s * hp * 4 + 2 * SUBLANES * s * 4 + 6 * tq * s * 4
    head = lambda i, j: (i, 0, j)
    return pl.pallas_call(
        functools.partial(_attn_kernel, tq=tq),
        grid=(b, nh),
        in_specs=[pl.BlockSpec((None, s, HEAD_DIM), head),
                  pl.BlockSpec((None, s, HEAD_DIM), head),
                  pl.BlockSpec((None, s, HEAD_DIM), lambda i, j: (i, 0, j + nh)),
                  pl.BlockSpec((None, s, hp), lambda i, j: (i, 0, 0)),
                  pl.BlockSpec((None, None, 1, s), lambda i, j: (i, j, 0, 0))],
        out_specs=pl.BlockSpec((None, s, HEAD_DIM), head),
        out_shape=jax.ShapeDtypeStruct((b, s, d), BF16),
        compiler_params=_params(vmem, 2),
        name="attention",
    )(q, kv, kv, cum, cum_k)


def _pad_last(x, n):
    return jnp.pad(x, [(0, 0)] * (x.ndim - 1) + [(0, n - x.shape[-1])])


def kernel(x, ln_mix, ln_ffn, pool_w, pool_scale, kv_norm, w_kvf, b_f, w_q, w_o, w_up, conv_w,
           conv_b, w_down, final_norm):
    b, s, d = x.shape
    m = b * s
    depth = ln_mix.shape[0]
    n_a = pool_w.shape[0]
    nh = d // HEAD_DIM
    f = w_down.shape[1]
    fp = _round_up(f, 1024)
    hp = _round_up(nh, LANES)

    w_gate, w_val = _prep_up(w_up, fp)
    w_dn = _prep_down(w_down, fp)
    w_kv = _prep_cols(w_kvf, 2 * d)
    w_q_b, w_o_b, pool_w_b = w_q.astype(BF16), w_o.astype(BF16), pool_w.astype(BF16)
    halves = lambda p: (_pad_last(p[..., :f], fp), _pad_last(p[..., f:], fp))

    def ffn(h2d, n2, layer, gains, **kw):
        cw_gate, cw_val = halves(conv_w[layer])
        cb_gate, cb_val = halves(conv_b[layer][None])
        act = _ffn_up(n2.reshape(b, s, d), w_gate, w_val, layer, cw_gate, cw_val, cb_gate, cb_val)
        return _mm_res_norm(act.reshape(m, fp), w_dn, layer, h2d, gains, **kw)

    h = x
    outs = None
    for layer in range(n_a):
        h, n2 = _pool_layer(h, ln_mix[layer][None], ln_ffn[layer][None], pool_scale[layer][None],
                            pool_w_b[layer])
        if layer + 1 < n_a:
            gains = jnp.zeros((0, d), F32)
        else:
            gains = jnp.stack([kv_norm, ln_mix[n_a]])
        outs = ffn(h.reshape(m, d), n2, layer, gains)
        h = outs[0].reshape(b, s, d)

    h2d, n_kv, n_q = outs
    kv = _mm(n_kv, w_kv, 0).reshape(b, s, 2 * d)
    cum = _fgate(n_kv.reshape(b, s, d), _pad_last(w_kvf[:, 2 * d:], hp).astype(BF16),
                 _pad_last(b_f[None, :], hp))
    cum_k = cum[:, :, :nh].transpose(0, 2, 1)[:, :, None, :]

    for j in range(depth - n_a):
        layer = n_a + j
        q = _mm(n_q, w_q_b, j, scale=HEAD_DIM ** -0.5 * LOG2E).reshape(b, s, d)
        o = _attention(q, kv, cum, cum_k)
        h2d, n2 = _mm_res_norm(o.reshape(m, d), w_o_b, j, h2d, ln_ffn[layer][None])
        if layer + 1 < depth:
            h2d, n_q = ffn(h2d, n2, layer, ln_mix[layer + 1][None])
        else:
            (out,) = ffn(h2d, n2, layer, final_norm[None], emit_h=False, norm_dtype=F32)
    return out.reshape(b, s, d)
```

```python
import functools
import math

import jax
import jax.numpy as jnp
from jax import lax
from jax.experimental import pallas as pl
from jax.experimental.pallas import tpu as pltpu

EPS = 1e-6
POOL_WINDOWS = (2, 4, 8, 16)
POOL_CARRY_ROWS = 16
HEAD_DIM = 128
CONV_WIDTH = 3
LOG2E = math.log2(math.e)
SUBLANES = 8
BF16_ROWS = 16
LANES = 128
W_BLOCK = 256
V7X_VMEM_BYTES = 64 * 1024 * 1024
VMEM_RESERVE_BYTES = 6 * 1024 * 1024

F32 = jnp.float32
BF16 = jnp.bfloat16


def _round_up(x, m):
    return (x + m - 1) // m * m


def _largest_tile(n, cap, quantum):
    if n <= cap:
        return n
    t = cap - cap % quantum
    while t >= quantum:
        if n % t == 0:
            return t
        t -= quantum
    raise ValueError(f"no tile of {n} is a multiple of {quantum} and <= {cap}")


def _params(vmem_estimate_bytes, n_grid_dims):
    limit = min(int(vmem_estimate_bytes * 5 // 4) + VMEM_RESERVE_BYTES,
                V7X_VMEM_BYTES - VMEM_RESERVE_BYTES)
    return pltpu.CompilerParams(
        dimension_semantics=("arbitrary",) * n_grid_dims,
        vmem_limit_bytes=limit,
    )


def _rms(x, g):
    ms = jnp.mean(x * x, axis=-1, keepdims=True)
    return (x * lax.rsqrt(ms + EPS)) * g


def _pool_kernel(h_ref, gm_ref, gf_ref, sc_ref, w_ref, ho_ref, n2_ref, carry_ref):
    s = pl.program_id(1)
    ts, d = h_ref.shape
    c = d // len(POOL_WINDOWS)

    @pl.when(s == 0)
    def _():
        carry_ref[...] = jnp.zeros_like(carry_ref)

    x = h_ref[...]
    n = _rms(x, gm_ref[...])
    pos = s * ts + lax.broadcasted_iota(jnp.int32, (ts, 1), 0)
    for g, w in enumerate(POOL_WINDOWS):
        cols = slice(g * c, (g + 1) * c)
        acc = jnp.concatenate([carry_ref[:, cols], n[:, cols]], axis=0)
        shift = 1
        while shift < w:
            acc = acc + pltpu.roll(acc, shift, 0)
            shift *= 2
        count = jnp.minimum(pos + 1, w).astype(F32)
        mix = (acc[POOL_CARRY_ROWS:, :] / count - n[:, cols]).astype(BF16)
        y = jnp.dot(mix, w_ref[g], preferred_element_type=F32)
        ho_ref[:, cols] = x[:, cols] + y * sc_ref[:, cols]
    carry_ref[...] = n[ts - POOL_CARRY_ROWS:, :]
    n2_ref[...] = _rms(ho_ref[...], gf_ref[...]).astype(n2_ref.dtype)


def _pool_layer(h, g_mix, g_ffn, scale, w_pool):
    b, s, d = h.shape
    ts = _largest_tile(s, 256, POOL_CARRY_ROWS)
    vmem = 2 * ts * d * (4 + 4 + 2) + 2 * w_pool.size * 2 + 8 * ts * d * 4
    row = pl.BlockSpec((None, ts, d), lambda i, j: (i, j, 0))
    vec = pl.BlockSpec((1, d), lambda i, j: (0, 0))
    return pl.pallas_call(
        _pool_kernel,
        grid=(b, s // ts),
        in_specs=[row, vec, vec, vec, pl.BlockSpec(w_pool.shape, lambda i, j: (0, 0, 0))],
        out_specs=[row, row],
        out_shape=[jax.ShapeDtypeStruct(h.shape, F32), jax.ShapeDtypeStruct(h.shape, BF16)],
        scratch_shapes=[pltpu.VMEM((POOL_CARRY_ROWS, d), F32)],
        compiler_params=_params(vmem, 2),
        name="pool_layer",
    )(h, g_mix, g_ffn, scale, w_pool)


def _cast_blocks_kernel(*refs, n_valid):
    j = pl.program_id(1)
    n = len(refs) // 2
    for x_ref, o_ref in zip(refs[:n], refs[n:]):
        o_ref[...] = jnp.where(j < n_valid, x_ref[...], 0.0).astype(o_ref.dtype)


def _cast_blocks(xs, grid, in_specs, out_specs, out_shapes, n_valid):
    block_elems = sum(math.prod(d for d in sp.block_shape if d is not None) for sp in in_specs)
    return pl.pallas_call(
        functools.partial(_cast_blocks_kernel, n_valid=n_valid),
        grid=grid,
        in_specs=in_specs,
        out_specs=out_specs,
        out_shape=[jax.ShapeDtypeStruct(sh, BF16) for sh in out_shapes],
        compiler_params=_params(2 * block_elems * (4 + 2) + block_elems * 4, 2),
        name="cast_blocks",
    )(*xs)


def _prep_up(w_up, fp):
    n_layers, d, f2 = w_up.shape
    nv, nb = f2 // 2 // W_BLOCK, fp // W_BLOCK
    src = lambda half: pl.BlockSpec(
        (None, d, W_BLOCK), lambda l, j: (l, 0, half * nv + jnp.minimum(j, nv - 1)))
    dst = pl.BlockSpec((None, None, d, W_BLOCK), lambda l, j: (l, j, 0, 0))
    return _cast_blocks([w_up, w_up], (n_layers, nb), [src(0), src(1)], [dst, dst],
                        [(n_layers, nb, d, W_BLOCK)] * 2, nv)


def _prep_down(w_down, fp):
    n_layers, f, d = w_down.shape
    nv, nb = f // W_BLOCK, fp // W_BLOCK
    src = pl.BlockSpec((None, W_BLOCK, d), lambda l, j: (l, jnp.minimum(j, nv - 1), 0))
    dst = pl.BlockSpec((None, W_BLOCK, d), lambda l, j: (l, j, 0))
    (out,) = _cast_blocks([w_down], (n_layers, nb), [src], [dst], [(n_layers, fp, d)], nv)
    return out


def _prep_cols(w, n_cols):
    r = w.shape[0]
    tr = _largest_tile(r, 512, BF16_ROWS)
    src = pl.BlockSpec((tr, n_cols), lambda l, j: (j, 0))
    dst = pl.BlockSpec((None, tr, n_cols), lambda l, j: (0, j, 0))
    (out,) = _cast_blocks([w], (1, r // tr), [src], [dst], [(1, r, n_cols)], r // tr)
    return out


def _causal_conv(u, tail, w, b):
    rows = lax.broadcasted_iota(jnp.int32, tail.shape, 0)

    def lagged(k):
        r = pltpu.roll(u, k, 0)
        head = jnp.where(rows < k, pltpu.roll(tail, k, 0), r[:SUBLANES])
        return jnp.concatenate([head, r[SUBLANES:]], axis=0)

    y = b + w[0:1, :] * u
    for k in range(1, CONV_WIDTH):
        y = y + w[k:k + 1, :] * lagged(k)
    return y


def _ffn_up_kernel(a_ref, wg_ref, wv_ref, cwg_ref, cwv_ref, cbg_ref, cbv_ref, o_ref, *, rc, n_last):
    s = o_ref.shape[0]
    nb, _, wb = wg_ref.shape
    zeros = jnp.zeros((SUBLANES, wb), F32)

    def body(n_blocks):
        tails = [(zeros, zeros)] * n_blocks
        for r0 in range(0, s, rc):
            a = a_ref[r0:r0 + rc, :]
            for t in range(n_blocks):
                cols = slice(t * wb, (t + 1) * wb)
                ug = jnp.dot(a, wg_ref[t], preferred_element_type=F32)
                uv = jnp.dot(a, wv_ref[t], preferred_element_type=F32)
                yg = _causal_conv(ug, tails[t][0], cwg_ref[:, cols], cbg_ref[:, cols])
                yv = _causal_conv(uv, tails[t][1], cwv_ref[:, cols], cbv_ref[:, cols])
                o_ref[r0:r0 + rc, cols] = (yg * jax.nn.sigmoid(yg) * yv).astype(o_ref.dtype)
                tails[t] = (ug[rc - SUBLANES:, :], uv[rc - SUBLANES:, :])
        if n_blocks < nb:
            o_ref[:, n_blocks * wb:] = jnp.zeros((s, (nb - n_blocks) * wb), o_ref.dtype)

    if n_last == nb:
        body(nb)
    else:
        last = pl.num_programs(1) - 1
        pl.when(pl.program_id(1) < last)(lambda: body(nb))
        pl.when(pl.program_id(1) == last)(lambda: body(n_last))


def _ffn_up(n2, w_gate, w_val, layer, n_valid, cw_gate, cw_val, cb_gate, cb_val):
    b, s, d = n2.shape
    fp = cw_gate.shape[1]
    nb = 2
    bn = nb * W_BLOCK
    n_last = n_valid - (fp // bn - 1) * nb
    assert 0 < n_last <= nb
    rc = _largest_tile(s, 256, SUBLANES)
    vmem = 2 * s * d * 2 + 4 * d * bn * 2 + 2 * s * bn * 2 + 4 * rc * bn * 4
    col = lambda i, j: (0, j)
    w_spec = pl.BlockSpec((None, nb, d, W_BLOCK), lambda i, j: (layer, j, 0, 0))
    cw_spec = pl.BlockSpec((CONV_WIDTH, bn), col)
    cb_spec = pl.BlockSpec((1, bn), col)
    return pl.pallas_call(
        functools.partial(_ffn_up_kernel, rc=rc, n_last=n_last),
        grid=(b, fp // bn),
        in_specs=[
            pl.BlockSpec((None, s, d), lambda i, j: (i, 0, 0)),
            w_spec, w_spec, cw_spec, cw_spec, cb_spec, cb_spec,
        ],
        out_specs=pl.BlockSpec((None, s, bn), lambda i, j: (i, 0, j)),
        out_shape=jax.ShapeDtypeStruct((b, s, fp), BF16),
        compiler_params=_params(vmem, 2),
        name="ffn_up",
    )(n2, w_gate, w_val, cw_gate, cw_val, cb_gate, cb_val)


def _mm_res_norm_kernel(a_ref, w_ref, h_ref, g_ref, *refs, ni, ne, cn, k_last, n_gains, emit_h):
    outs, acc_ref = refs[:-1], refs[-1]
    i, k = pl.program_id(0), pl.program_id(1)
    nk = pl.num_programs(1)
    _, bm, d = acc_ref.shape
    bk = a_ref.shape[1]
    rs = bm // ne

    def accumulate(kw, first):
        cur = i % 2
        for c in range(0, d, cn):
            part = jnp.dot(a_ref[:, :kw], w_ref[:kw, c:c + cn], preferred_element_type=F32)
            if first:
                acc_ref[cur, :, c:c + cn] = part
            else:
                acc_ref[cur, :, c:c + cn] += part

    live = i < ni
    if k_last == bk or nk == 1:
        pl.when(live & (k == 0))(lambda: accumulate(k_last if nk == 1 else bk, True))
        pl.when(live & (k > 0))(lambda: accumulate(bk, False))
    else:
        pl.when(live & (k == 0))(lambda: accumulate(bk, True))
        pl.when(live & (k > 0) & (k < nk - 1))(lambda: accumulate(bk, False))
        pl.when(live & (k == nk - 1))(lambda: accumulate(k_last, False))

    @pl.when((i > 0) & (k < ne))
    def _():
        r0 = pl.multiple_of(k * rs, rs)
        hn = acc_ref[(i - 1) % 2, pl.ds(r0, rs), :] + h_ref[...]
        if emit_h:
            outs[0][...] = hn
        if n_gains:
            z = hn * lax.rsqrt(jnp.mean(hn * hn, axis=-1, keepdims=True) + EPS)
            for t in range(n_gains):
                o = outs[t + int(emit_h)]
                o[...] = (z * g_ref[t:t + 1, :]).astype(o.dtype)


def _mm_res_norm(a, w, layer, h, gains, *, k_valid=None, emit_h=True, norm_dtype=BF16):
    m, kdim = a.shape
    d = w.shape[2]
    n_gains = gains.shape[0]
    bm = _largest_tile(m, 1024, BF16_ROWS)
    bk = _largest_tile(kdim, 512, LANES)
    cn = _largest_tile(d, 512, LANES)
    ni, nk = m // bm, kdim // bk
    k_last = (kdim if k_valid is None else k_valid) - (nk - 1) * bk
    assert 0 < k_last <= bk and k_last % LANES == 0
    ne = 1
    while ne * 2 <= nk and bm // (ne * 2) >= 64:
        ne *= 2
    rs = bm // ne
    norm_bytes = jnp.dtype(norm_dtype).itemsize
    vmem = (2 * bm * d * 4 + 2 * bm * bk * 2 + 2 * bk * d * 2 + 2 * bm * cn * 4
            + rs * d * (2 * 4 + 2 * 4 * int(emit_h) + 2 * norm_bytes * n_gains + 4 * 4))

    def sub_block(i, k):
        return (jnp.maximum(i - 1, 0) * ne + jnp.where(i == 0, 0, jnp.minimum(k, ne - 1)), 0)

    row = pl.BlockSpec((rs, d), sub_block)
    out_shape = ([jax.ShapeDtypeStruct((m, d), F32)] if emit_h else []) + \
        [jax.ShapeDtypeStruct((m, d), norm_dtype)] * n_gains
    g_in = gains if n_gains else jnp.zeros((1, d), F32)
    return pl.pallas_call(
        functools.partial(_mm_res_norm_kernel, ni=ni, ne=ne, cn=cn, k_last=k_last, n_gains=n_gains,
                          emit_h=emit_h),
        grid=(ni + 1, nk),
        in_specs=[
            pl.BlockSpec((bm, bk), lambda i, k: (jnp.minimum(i, ni - 1), jnp.where(i < ni, k, nk - 1))),
            pl.BlockSpec((None, bk, d), lambda i, k: (layer, jnp.where(i < ni, k, nk - 1), 0)),
            row,
            pl.BlockSpec(g_in.shape, lambda i, k: (0, 0)),
        ],
        out_specs=[row] * len(out_shape),
        out_shape=out_shape,
        scratch_shapes=[pltpu.VMEM((2, bm, d), F32)],
        compiler_params=_params(vmem, 2),
        name="mm_res_norm",
    )(a, w, h, g_in)


def _mm_kernel(a_ref, w_ref, o_ref, *, scale):
    acc = jnp.dot(a_ref[...], w_ref[...], preferred_element_type=F32)
    if scale is not None:
        acc = acc * scale
    o_ref[...] = acc.astype(o_ref.dtype)


def _mm(a, w, layer, scale=None):
    m, kdim = a.shape
    n = w.shape[2]
    bm = _largest_tile(m, 1024, BF16_ROWS)
    bn = _largest_tile(n, 1024, LANES)
    vmem = 2 * bm * kdim * 2 + 2 * kdim * bn * 2 + 2 * bm * bn * 2 + bm * bn * 4
    return pl.pallas_call(
        functools.partial(_mm_kernel, scale=scale),
        grid=(m // bm, n // bn),
        in_specs=[pl.BlockSpec((bm, kdim), lambda i, j: (i, 0)),
                  pl.BlockSpec((None, kdim, bn), lambda i, j: (layer, 0, j))],
        out_specs=pl.BlockSpec((bm, bn), lambda i, j: (i, j)),
        out_shape=jax.ShapeDtypeStruct((m, n), BF16),
        compiler_params=_params(vmem, 2),
        name="mm",
    )(a, w)


def _fgate_kernel(a_ref, w_ref, b_ref, o_ref):
    x = jnp.dot(a_ref[...], w_ref[...], preferred_element_type=F32) + b_ref[...]
    c = jnp.minimum(x, 0.0) - jnp.log1p(jnp.exp(-jnp.abs(x)))
    s = c.shape[0]
    rows = lax.broadcasted_iota(jnp.int32, c.shape, 0)
    shift = 1
    while shift < s:
        c = c + jnp.where(rows >= shift, pltpu.roll(c, shift, 0), 0.0)
        shift *= 2
    o_ref[...] = c * LOG2E


def _fgate(n_kv, w_f, b_f):
    b, s, d = n_kv.shape
    hp = w_f.shape[1]
    vmem = 2 * s * d * 2 + 2 * d * hp * 2 + 6 * s * hp * 4
    return pl.pallas_call(
        _fgate_kernel,
        grid=(b,),
        in_specs=[pl.BlockSpec((None, s, d), lambda i: (i, 0, 0)),
                  pl.BlockSpec((d, hp), lambda i: (0, 0)),
                  pl.BlockSpec((1, hp), lambda i: (0, 0))],
        out_specs=pl.BlockSpec((None, s, hp), lambda i: (i, 0, 0)),
        out_shape=jax.ShapeDtypeStruct((b, s, hp), F32),
        compiler_params=_params(vmem, 1),
        name="fgate",
    )(n_kv, w_f, b_f)


def _attn_kernel(q_ref, k_ref, v_ref, cum_ref, ck_ref, o_ref, *, tq):
    s = q_ref.shape[0]
    head_lane = lax.broadcasted_iota(jnp.int32, (tq, cum_ref.shape[1]), 1) == pl.program_id(1)
    nt = (((1,), (1,)), ((), ()))
    neg = jnp.finfo(F32).min
    row = lax.broadcasted_iota(jnp.int32, (tq, tq), 0)
    col = lax.broadcasted_iota(jnp.int32, (tq, tq), 1)
    for i in reversed(range(s // tq)):
        q0, q1 = i * tq, (i + 1) * tq
        q = q_ref[q0:q1, :]
        cq = jnp.sum(jnp.where(head_lane, cum_ref[q0:q1, :], 0.0), axis=-1, keepdims=True)
        diag = lax.dot_general(q, k_ref[q0:q1, :], nt, preferred_element_type=F32)
        diag = jnp.where(col <= row, diag + cq - ck_ref[:, q0:q1], neg)
        m = jnp.max(diag, axis=-1, keepdims=True)
        if i > 0:
            past = lax.dot_general(q, k_ref[0:q0, :], nt, preferred_element_type=F32)
            past = past + cq - ck_ref[:, 0:q0]
            m = jnp.maximum(m, jnp.max(past, axis=-1, keepdims=True))
        p = jnp.exp2(diag - m)
        l = jnp.sum(p, axis=-1, keepdims=True)
        o = jnp.dot(p.astype(BF16), v_ref[q0:q1, :], preferred_element_type=F32)
        if i > 0:
            p = jnp.exp2(past - m)
            l = l + jnp.sum(p, axis=-1, keepdims=True)
            o = o + jnp.dot(p.astype(BF16), v_ref[0:q0, :], preferred_element_type=F32)
        o_ref[q0:q1, :] = (o / l).astype(o_ref.dtype)


def _attention(q, kv, cum, cum_k):
    b, s, d = q.shape
    nh = d // HEAD_DIM
    hp = cum.shape[2]
    tq = _largest_tile(s, 512, LANES)
    vmem = 2 * 4 * s * HEAD_DIM * 2 + 2 * s * hp * 4 + 2 * SUBLANES * s * 4 + 6 * tq * s * 4
    head = lambda i, j: (i, 0, j)
    return pl.pallas_call(
        functools.partial(_attn_kernel, tq=tq),
        grid=(b, nh),
        in_specs=[pl.BlockSpec((None, s, HEAD_DIM), head),
                  pl.BlockSpec((None, s, HEAD_DIM), head),
                  pl.BlockSpec((None, s, HEAD_DIM), lambda i, j: (i, 0, j + nh)),
                  pl.BlockSpec((None, s, hp), lambda i, j: (i, 0, 0)),
                  pl.BlockSpec((None, None, 1, s), lambda i, j: (i, j, 0, 0))],
        out_specs=pl.BlockSpec((None, s, HEAD_DIM), head),
        out_shape=jax.ShapeDtypeStruct((b, s, d), BF16),
        compiler_params=_params(vmem, 2),
        name="attention",
    )(q, kv, kv, cum, cum_k)


def _pad_last(x, n):
    return jnp.pad(x, [(0, 0)] * (x.ndim - 1) + [(0, n - x.shape[-1])])


def kernel(x, ln_mix, ln_ffn, pool_w, pool_scale, kv_norm, w_kvf, b_f, w_q, w_o, w_up, conv_w,
           conv_b, w_down, final_norm):
    b, s, d = x.shape
    m = b * s
    depth = ln_mix.shape[0]
    n_a = pool_w.shape[0]
    nh = d // HEAD_DIM
    f = w_down.shape[1]
    fp = _round_up(f, 2 * W_BLOCK)
    hp = _round_up(nh, LANES)

    w_gate, w_val = _prep_up(w_up, fp)
    w_dn = _prep_down(w_down, fp)
    w_kv = _prep_cols(w_kvf, 2 * d)
    w_q_b, w_o_b, pool_w_b = w_q.astype(BF16), w_o.astype(BF16), pool_w.astype(BF16)
    halves = lambda p: (_pad_last(p[..., :f], fp), _pad_last(p[..., f:], fp))

    def ffn(h2d, n2, layer, gains, **kw):
        cw_gate, cw_val = halves(conv_w[layer])
        cb_gate, cb_val = halves(conv_b[layer][None])
        act = _ffn_up(n2.reshape(b, s, d), w_gate, w_val, layer, f // W_BLOCK, cw_gate, cw_val,
                      cb_gate, cb_val)
        return _mm_res_norm(act.reshape(m, fp), w_dn, layer, h2d, gains, k_valid=f, **kw)

    h = x
    outs = None
    for layer in range(n_a):
        h, n2 = _pool_layer(h, ln_mix[layer][None], ln_ffn[layer][None], pool_scale[layer][None],
                            pool_w_b[layer])
        if layer + 1 < n_a:
            gains = jnp.zeros((0, d), F32)
        else:
            gains = jnp.stack([kv_norm, ln_mix[n_a]])
        outs = ffn(h.reshape(m, d), n2, layer, gains)
        h = outs[0].reshape(b, s, d)

    h2d, n_kv, n_q = outs
    kv = _mm(n_kv, w_kv, 0).reshape(b, s, 2 * d)
    cum = _fgate(n_kv.reshape(b, s, d), _pad_last(w_kvf[:, 2 * d:], hp).astype(BF16),
                 _pad_last(b_f[None, :], hp))
    cum_k = cum[:, :, :nh].transpose(0, 2, 1)[:, :, None, :]

    for j in range(depth - n_a):
        layer = n_a + j
        q = _mm(n_q, w_q_b, j, scale=HEAD_DIM ** -0.5 * LOG2E).reshape(b, s, d)
        o = _attention(q, kv, cum, cum_k)
        h2d, n2 = _mm_res_norm(o.reshape(m, d), w_o_b, j, h2d, ln_ffn[layer][None])
        if layer + 1 < depth:
            h2d, n_q = ffn(h2d, n2, layer, ln_mix[layer + 1][None])
        else:
            (out,) = ffn(h2d, n2, layer, final_norm[None], emit_h=False, norm_dtype=F32)
    return out.reshape(b, s, d)
```

```python
import functools
import math

import jax
import jax.numpy as jnp
from jax import lax
from jax.experimental import pallas as pl
from jax.experimental.pallas import tpu as pltpu

EPS = 1e-6
POOL_WINDOWS = (2, 4, 8, 16)
POOL_CARRY_ROWS = 16
HEAD_DIM = 128
CONV_WIDTH = 3
LOG2E = math.log2(math.e)
SUBLANES = 8
BF16_ROWS = 16
LANES = 128
W_BLOCK = 256
V7X_VMEM_BYTES = 64 * 1024 * 1024
VMEM_RESERVE_BYTES = 6 * 1024 * 1024

F32 = jnp.float32
BF16 = jnp.bfloat16


def _round_up(x, m):
    return (x + m - 1) // m * m


def _largest_tile(n, cap, quantum):
    if n <= cap:
        return n
    t = cap - cap % quantum
    while t >= quantum:
        if n % t == 0:
            return t
        t -= quantum
    raise ValueError(f"no tile of {n} is a multiple of {quantum} and <= {cap}")


def _params(vmem_estimate_bytes, n_grid_dims):
    limit = min(int(vmem_estimate_bytes * 5 // 4) + VMEM_RESERVE_BYTES,
                V7X_VMEM_BYTES - VMEM_RESERVE_BYTES)
    return pltpu.CompilerParams(
        dimension_semantics=("arbitrary",) * n_grid_dims,
        vmem_limit_bytes=limit,
    )


def _rms(x, g):
    ms = jnp.mean(x * x, axis=-1, keepdims=True)
    return (x * lax.rsqrt(ms + EPS)) * g


def _pool_kernel(h_ref, gm_ref, gf_ref, sc_ref, w_ref, ho_ref, n2_ref, carry_ref):
    s = pl.program_id(1)
    ts, d = h_ref.shape
    c = d // len(POOL_WINDOWS)

    @pl.when(s == 0)
    def _():
        carry_ref[...] = jnp.zeros_like(carry_ref)

    x = h_ref[...]
    n = _rms(x, gm_ref[...])
    pos = s * ts + lax.broadcasted_iota(jnp.int32, (ts, 1), 0)
    for g, w in enumerate(POOL_WINDOWS):
        cols = slice(g * c, (g + 1) * c)
        acc = jnp.concatenate([carry_ref[:, cols], n[:, cols]], axis=0)
        shift = 1
        while shift < w:
            acc = acc + pltpu.roll(acc, shift, 0)
            shift *= 2
        count = jnp.minimum(pos + 1, w).astype(F32)
        mix = (acc[POOL_CARRY_ROWS:, :] / count - n[:, cols]).astype(BF16)
        y = jnp.dot(mix, w_ref[g], preferred_element_type=F32)
        ho_ref[:, cols] = x[:, cols] + y * sc_ref[:, cols]
    carry_ref[...] = n[ts - POOL_CARRY_ROWS:, :]
    n2_ref[...] = _rms(ho_ref[...], gf_ref[...]).astype(n2_ref.dtype)


def _pool_layer(h, g_mix, g_ffn, scale, w_pool):
    b, s, d = h.shape
    ts = _largest_tile(s, 256, POOL_CARRY_ROWS)
    vmem = 2 * ts * d * (4 + 4 + 2) + 2 * w_pool.size * 2 + 8 * ts * d * 4
    row = pl.BlockSpec((None, ts, d), lambda i, j: (i, j, 0))
    vec = pl.BlockSpec((1, d), lambda i, j: (0, 0))
    return pl.pallas_call(
        _pool_kernel,
        grid=(b, s // ts),
        in_specs=[row, vec, vec, vec, pl.BlockSpec(w_pool.shape, lambda i, j: (0, 0, 0))],
        out_specs=[row, row],
        out_shape=[jax.ShapeDtypeStruct(h.shape, F32), jax.ShapeDtypeStruct(h.shape, BF16)],
        scratch_shapes=[pltpu.VMEM((POOL_CARRY_ROWS, d), F32)],
        compiler_params=_params(vmem, 2),
        name="pool_layer",
    )(h, g_mix, g_ffn, scale, w_pool)


def _cast_blocks_kernel(*refs, n_valid):
    *x_refs, o_ref = refs
    j = pl.program_id(1)
    cast = lambda x_ref: jnp.where(j < n_valid, x_ref[...], 0.0).astype(o_ref.dtype)
    if len(x_refs) == 1:
        o_ref[...] = cast(x_refs[0])
    else:
        for t, x_ref in enumerate(x_refs):
            o_ref[t] = cast(x_ref)


def _cast_blocks(xs, grid, in_specs, out_spec, out_shape, n_valid):
    block_elems = sum(math.prod(d for d in sp.block_shape if d is not None) for sp in in_specs)
    return pl.pallas_call(
        functools.partial(_cast_blocks_kernel, n_valid=n_valid),
        grid=grid,
        in_specs=in_specs,
        out_specs=out_spec,
        out_shape=jax.ShapeDtypeStruct(out_shape, BF16),
        compiler_params=_params(2 * block_elems * (4 + 2) + block_elems * 4, 2),
        name="cast_blocks",
    )(*xs)


def _prep_up(w_up, fp):
    n_layers, d, f2 = w_up.shape
    nv, nb = f2 // 2 // W_BLOCK, fp // W_BLOCK
    src = lambda half: pl.BlockSpec(
        (None, d, W_BLOCK), lambda l, j: (l, 0, half * nv + jnp.minimum(j, nv - 1)))
    dst = pl.BlockSpec((None, None, 2, d, W_BLOCK), lambda l, j: (l, j, 0, 0, 0))
    return _cast_blocks([w_up, w_up], (n_layers, nb), [src(0), src(1)], dst,
                        (n_layers, nb, 2, d, W_BLOCK), nv)


def _prep_down(w_down, fp):
    n_layers, f, d = w_down.shape
    nv, nb = f // W_BLOCK, fp // W_BLOCK
    src = pl.BlockSpec((None, W_BLOCK, d), lambda l, j: (l, jnp.minimum(j, nv - 1), 0))
    dst = pl.BlockSpec((None, W_BLOCK, d), lambda l, j: (l, j, 0))
    return _cast_blocks([w_down], (n_layers, nb), [src], dst, (n_layers, fp, d), nv)


def _prep_rows(w, n_rows):
    c = w.shape[1]
    tr = _largest_tile(n_rows, 512, BF16_ROWS)
    src = pl.BlockSpec((tr, c), lambda l, j: (j, 0))
    dst = pl.BlockSpec((None, tr, c), lambda l, j: (0, j, 0))
    return _cast_blocks([w], (1, n_rows // tr), [src], dst, (1, n_rows, c), n_rows // tr)


def _causal_conv(u, tail, w, b):
    rows = lax.broadcasted_iota(jnp.int32, tail.shape, 0)

    def lagged(k):
        r = pltpu.roll(u, k, 0)
        head = jnp.where(rows < k, pltpu.roll(tail, k, 0), r[:SUBLANES])
        return jnp.concatenate([head, r[SUBLANES:]], axis=0)

    y = b + w[0:1, :] * u
    for k in range(1, CONV_WIDTH):
        y = y + w[k:k + 1, :] * lagged(k)
    return y


def _ffn_up_kernel(a_ref, w_ref, cp_ref, o_ref, *, rc, n_last):
    s = o_ref.shape[0]
    nb, _, _, wb = w_ref.shape
    cw = CONV_WIDTH
    zeros = jnp.zeros((SUBLANES, wb), F32)

    def body(n_blocks):
        tails = [(zeros, zeros)] * n_blocks
        for r0 in range(0, s, rc):
            a = a_ref[r0:r0 + rc, :]
            for t in range(n_blocks):
                cols = slice(t * wb, (t + 1) * wb)
                ug = jnp.dot(a, w_ref[t, 0], preferred_element_type=F32)
                uv = jnp.dot(a, w_ref[t, 1], preferred_element_type=F32)
                cp = cp_ref[:, cols]
                yg = _causal_conv(ug, tails[t][0], cp[0:cw], cp[cw:cw + 1])
                yv = _causal_conv(uv, tails[t][1], cp[cw + 1:2 * cw + 1], cp[2 * cw + 1:2 * cw + 2])
                o_ref[r0:r0 + rc, cols] = (yg * jax.nn.sigmoid(yg) * yv).astype(o_ref.dtype)
                tails[t] = (ug[rc - SUBLANES:, :], uv[rc - SUBLANES:, :])
        if n_blocks < nb:
            o_ref[:, n_blocks * wb:] = jnp.zeros((s, (nb - n_blocks) * wb), o_ref.dtype)

    if n_last == nb:
        body(nb)
    else:
        last = pl.num_programs(1) - 1
        pl.when(pl.program_id(1) < last)(lambda: body(nb))
        pl.when(pl.program_id(1) == last)(lambda: body(n_last))


def _ffn_up(n2, w_up, layer, n_valid, conv_params):
    b, s, d = n2.shape
    fp = conv_params.shape[1]
    nb = 2
    bn = nb * W_BLOCK
    n_last = n_valid - (fp // bn - 1) * nb
    assert 0 < n_last <= nb
    rc = _largest_tile(s, 256, SUBLANES)
    vmem = 2 * s * d * 2 + 4 * d * bn * 2 + 2 * s * bn * 2 + 4 * rc * bn * 4
    return pl.pallas_call(
        functools.partial(_ffn_up_kernel, rc=rc, n_last=n_last),
        grid=(b, fp // bn),
        in_specs=[
            pl.BlockSpec((None, s, d), lambda i, j: (i, 0, 0)),
            pl.BlockSpec((None, nb, 2, d, W_BLOCK), lambda i, j: (layer, j, 0, 0, 0)),
            pl.BlockSpec(conv_params.shape[:1] + (bn,), lambda i, j: (0, j)),
        ],
        out_specs=pl.BlockSpec((None, s, bn), lambda i, j: (i, 0, j)),
        out_shape=jax.ShapeDtypeStruct((b, s, fp), BF16),
        compiler_params=_params(vmem, 2),
        name="ffn_up",
    )(n2, w_up, conv_params)


def _mm_res_norm_kernel(a_ref, w_ref, h_ref, g_ref, *refs, ni, ne, cn, k_last, n_gains, emit_h):
    outs, acc_ref = refs[:-1], refs[-1]
    i, k = pl.program_id(0), pl.program_id(1)
    nk = pl.num_programs(1)
    _, bm, d = acc_ref.shape
    bk = a_ref.shape[1]
    rs = bm // ne

    def accumulate(kw, first):
        cur = i % 2
        for c in range(0, d, cn):
            part = jnp.dot(a_ref[:, :kw], w_ref[:kw, c:c + cn], preferred_element_type=F32)
            if first:
                acc_ref[cur, :, c:c + cn] = part
            else:
                acc_ref[cur, :, c:c + cn] += part

    live = i < ni
    if k_last == bk or nk == 1:
        pl.when(live & (k == 0))(lambda: accumulate(k_last if nk == 1 else bk, True))
        pl.when(live & (k > 0))(lambda: accumulate(bk, False))
    else:
        pl.when(live & (k == 0))(lambda: accumulate(bk, True))
        pl.when(live & (k > 0) & (k < nk - 1))(lambda: accumulate(bk, False))
        pl.when(live & (k == nk - 1))(lambda: accumulate(k_last, False))

    @pl.when((i > 0) & (k < ne))
    def _():
        r0 = pl.multiple_of(k * rs, rs)
        hn = acc_ref[(i - 1) % 2, pl.ds(r0, rs), :] + h_ref[...]
        if emit_h:
            outs[0][...] = hn
        if n_gains:
            z = hn * lax.rsqrt(jnp.mean(hn * hn, axis=-1, keepdims=True) + EPS)
            for t in range(n_gains):
                o = outs[t + int(emit_h)]
                o[...] = (z * g_ref[t:t + 1, :]).astype(o.dtype)


def _mm_res_norm(a, w, layer, h, gains, *, k_valid=None, emit_h=True, norm_dtype=BF16):
    m, kdim = a.shape
    d = w.shape[2]
    n_gains = gains.shape[0]
    bm = _largest_tile(m, 1024, BF16_ROWS)
    bk = _largest_tile(kdim, 512, LANES)
    cn = _largest_tile(d, 512, LANES)
    ni, nk = m // bm, kdim // bk
    k_last = (kdim if k_valid is None else k_valid) - (nk - 1) * bk
    assert 0 < k_last <= bk and k_last % LANES == 0
    ne = 1
    while ne * 2 <= nk and bm // (ne * 2) >= 64:
        ne *= 2
    rs = bm // ne
    norm_bytes = jnp.dtype(norm_dtype).itemsize
    vmem = (2 * bm * d * 4 + 2 * bm * bk * 2 + 2 * bk * d * 2 + 2 * bm * cn * 4
            + rs * d * (2 * 4 + 2 * 4 * int(emit_h) + 2 * norm_bytes * n_gains + 4 * 4))

    def sub_block(i, k):
        return (jnp.maximum(i - 1, 0) * ne + jnp.where(i == 0, 0, jnp.minimum(k, ne - 1)), 0)

    row = pl.BlockSpec((rs, d), sub_block)
    out_shape = ([jax.ShapeDtypeStruct((m, d), F32)] if emit_h else []) + \
        [jax.ShapeDtypeStruct((m, d), norm_dtype)] * n_gains
    g_in = gains if n_gains else jnp.zeros((1, d), F32)
    return pl.pallas_call(
        functools.partial(_mm_res_norm_kernel, ni=ni, ne=ne, cn=cn, k_last=k_last, n_gains=n_gains,
                          emit_h=emit_h),
        grid=(ni + 1, nk),
        in_specs=[
            pl.BlockSpec((bm, bk), lambda i, k: (jnp.minimum(i, ni - 1), jnp.where(i < ni, k, nk - 1))),
            pl.BlockSpec((None, bk, d), lambda i, k: (layer, jnp.where(i < ni, k, nk - 1), 0)),
            row,
            pl.BlockSpec(g_in.shape, lambda i, k: (0, 0)),
        ],
        out_specs=[row] * len(out_shape),
        out_shape=out_shape,
        scratch_shapes=[pltpu.VMEM((2, bm, d), F32)],
        compiler_params=_params(vmem, 2),
        name="mm_res_norm",
    )(a, w, h, g_in)


def _mm_kernel(a_ref, w_ref, o_ref, *, scale, w_is_transposed):
    dims = (((1,), (1 if w_is_transposed else 0,)), ((), ()))
    acc = lax.dot_general(a_ref[...], w_ref[...], dims, preferred_element_type=F32)
    if scale is not None:
        acc = acc * scale
    o_ref[...] = acc.astype(o_ref.dtype)


def _mm(a, w, layer, scale=None, w_is_transposed=False):
    m, kdim = a.shape
    n = w.shape[1 if w_is_transposed else 2]
    bm = _largest_tile(m, 1024, BF16_ROWS)
    bn = _largest_tile(n, 1024, LANES)
    vmem = 2 * bm * kdim * 2 + 2 * kdim * bn * 2 + 2 * bm * bn * 2 + bm * bn * 4
    return pl.pallas_call(
        functools.partial(_mm_kernel, scale=scale, w_is_transposed=w_is_transposed),
        grid=(m // bm, n // bn),
        in_specs=[pl.BlockSpec((bm, kdim), lambda i, j: (i, 0)),
                  pl.BlockSpec((None, bn, kdim), lambda i, j: (layer, j, 0)) if w_is_transposed
                  else pl.BlockSpec((None, kdim, bn), lambda i, j: (layer, 0, j))],
        out_specs=pl.BlockSpec((bm, bn), lambda i, j: (i, j)),
        out_shape=jax.ShapeDtypeStruct((m, n), BF16),
        compiler_params=_params(vmem, 2),
        name="mm",
    )(a, w)


def _fgate_kernel(a_ref, w_ref, b_ref, o_ref):
    x = jnp.dot(a_ref[...], w_ref[...], preferred_element_type=F32) + b_ref[...]
    c = jnp.minimum(x, 0.0) - jnp.log1p(jnp.exp(-jnp.abs(x)))
    s = c.shape[0]
    rows = lax.broadcasted_iota(jnp.int32, c.shape, 0)
    shift = 1
    while shift < s:
        c = c + jnp.where(rows >= shift, pltpu.roll(c, shift, 0), 0.0)
        shift *= 2
    o_ref[...] = c * LOG2E


def _fgate(n_kv, w_f, b_f):
    b, s, d = n_kv.shape
    hp = w_f.shape[1]
    vmem = 2 * s * d * 2 + 2 * d * hp * 2 + 6 * s * hp * 4
    return pl.pallas_call(
        _fgate_kernel,
        grid=(b,),
        in_specs=[pl.BlockSpec((None, s, d), lambda i: (i, 0, 0)),
                  pl.BlockSpec((d, hp), lambda i: (0, 0)),
                  pl.BlockSpec((1, hp), lambda i: (0, 0))],
        out_specs=pl.BlockSpec((None, s, hp), lambda i: (i, 0, 0)),
        out_shape=jax.ShapeDtypeStruct((b, s, hp), F32),
        compiler_params=_params(vmem, 1),
        name="fgate",
    )(n_kv, w_f, b_f)


def _attn_kernel(q_ref, k_ref, v_ref, cum_ref, ck_ref, o_ref, *, tq):
    s = q_ref.shape[0]
    head_lane = lax.broadcasted_iota(jnp.int32, (tq, cum_ref.shape[1]), 1) == pl.program_id(1)
    nt = (((1,), (1,)), ((), ()))
    neg = jnp.finfo(F32).min
    row = lax.broadcasted_iota(jnp.int32, (tq, tq), 0)
    col = lax.broadcasted_iota(jnp.int32, (tq, tq), 1)
    for i in reversed(range(s // tq)):
        q0, q1 = i * tq, (i + 1) * tq
        q = q_ref[q0:q1, :]
        cq = jnp.sum(jnp.where(head_lane, cum_ref[q0:q1, :], 0.0), axis=-1, keepdims=True)
        diag = lax.dot_general(q, k_ref[q0:q1, :], nt, preferred_element_type=F32)
        diag = jnp.where(col <= row, diag + cq - ck_ref[:, q0:q1], neg)
        m = jnp.max(diag, axis=-1, keepdims=True)
        if i > 0:
            past = lax.dot_general(q, k_ref[0:q0, :], nt, preferred_element_type=F32)
            past = past + cq - ck_ref[:, 0:q0]
            m = jnp.maximum(m, jnp.max(past, axis=-1, keepdims=True))
        p = jnp.exp2(diag - m)
        l = jnp.sum(p, axis=-1, keepdims=True)
        o = jnp.dot(p.astype(BF16), v_ref[q0:q1, :], preferred_element_type=F32)
        if i > 0:
            p = jnp.exp2(past - m)
            l = l + jnp.sum(p, axis=-1, keepdims=True)
            o = o + jnp.dot(p.astype(BF16), v_ref[0:q0, :], preferred_element_type=F32)
        o_ref[q0:q1, :] = (o / l).astype(o_ref.dtype)


def _attention(q, kv, cum, cum_k):
    b, s, d = q.shape
    nh = d // HEAD_DIM
    hp = cum.shape[2]
    tq = _largest_tile(s, 512, LANES)
    vmem = 2 * 4 * s * HEAD_DIM * 2 + 2 * s * hp * 4 + 2 * SUBLANES * s * 4 + 6 * tq * s * 4
    head = lambda i, j: (i, 0, j)
    return pl.pallas_call(
        functools.partial(_attn_kernel, tq=tq),
        grid=(b, nh),
        in_specs=[pl.BlockSpec((None, s, HEAD_DIM), head),
                  pl.BlockSpec((None, s, HEAD_DIM), head),
                  pl.BlockSpec((None, s, HEAD_DIM), lambda i, j: (i, 0, j + nh)),
                  pl.BlockSpec((None, s, hp), lambda i, j: (i, 0, 0)),
                  pl.BlockSpec((None, None, 1, s), lambda i, j: (i, j, 0, 0))],
        out_specs=pl.BlockSpec((None, s, HEAD_DIM), head),
        out_shape=jax.ShapeDtypeStruct((b, s, d), BF16),
        compiler_params=_params(vmem, 2),
        name="attention",
    )(q, kv, kv, cum, cum_k)


def _pad_last(x, n):
    return jnp.pad(x, [(0, 0)] * (x.ndim - 1) + [(0, n - x.shape[-1])])


def kernel(x, ln_mix, ln_ffn, pool_w, pool_scale, kv_norm, w_kvf, b_f, w_q, w_o, w_up, conv_w,
           conv_b, w_down, final_norm):
    b, s, d = x.shape
    m = b * s
    depth = ln_mix.shape[0]
    n_a = pool_w.shape[0]
    nh = d // HEAD_DIM
    f = w_down.shape[1]
    fp = _round_up(f, 2 * W_BLOCK)
    hp = _round_up(nh, LANES)

    w_up_b = _prep_up(w_up, fp)
    w_dn = _prep_down(w_down, fp)
    w_kvf_t = w_kvf.T
    w_kv_t = _prep_rows(w_kvf_t, 2 * d)
    w_q_b, w_o_b, pool_w_b = w_q.astype(BF16), w_o.astype(BF16), pool_w.astype(BF16)
    halves = lambda p: (_pad_last(p[..., :f], fp), _pad_last(p[..., f:], fp))

    def ffn(h2d, n2, layer, gains, **kw):
        cw_gate, cw_val = halves(conv_w[layer])
        cb_gate, cb_val = halves(conv_b[layer][None])
        conv_params = jnp.concatenate([cw_gate, cb_gate, cw_val, cb_val], axis=0)
        act = _ffn_up(n2.reshape(b, s, d), w_up_b, layer, f // W_BLOCK, conv_params)
        return _mm_res_norm(act.reshape(m, fp), w_dn, layer, h2d, gains, k_valid=f, **kw)

    h = x
    outs = None
    for layer in range(n_a):
        h, n2 = _pool_layer(h, ln_mix[layer][None], ln_ffn[layer][None], pool_scale[layer][None],
                            pool_w_b[layer])
        if layer + 1 < n_a:
            gains = jnp.zeros((0, d), F32)
        else:
            gains = jnp.stack([kv_norm, ln_mix[n_a]])
        outs = ffn(h.reshape(m, d), n2, layer, gains)
        h = outs[0].reshape(b, s, d)

    h2d, n_kv, n_q = outs
    kv = _mm(n_kv, w_kv_t, 0, w_is_transposed=True).reshape(b, s, 2 * d)
    cum = _fgate(n_kv.reshape(b, s, d), _pad_last(w_kvf[:, 2 * d:], hp).astype(BF16),
                 _pad_last(b_f[None, :], hp))
    cum_k = cum[:, :, :nh].transpose(0, 2, 1)[:, :, None, :]

    for j in range(depth - n_a):
        layer = n_a + j
        q = _mm(n_q, w_q_b, j, scale=HEAD_DIM ** -0.5 * LOG2E).reshape(b, s, d)
        o = _attention(q, kv, cum, cum_k)
        h2d, n2 = _mm_res_norm(o.reshape(m, d), w_o_b, j, h2d, ln_ffn[layer][None])
        if layer + 1 < depth:
            h2d, n_q = ffn(h2d, n2, layer, ln_mix[layer + 1][None])
        else:
            (out,) = ffn(h2d, n2, layer, final_norm[None], emit_h=False, norm_dtype=F32)
    return out.reshape(b, s, d)
```

```python
import functools
import math

import jax
import jax.numpy as jnp
from jax import lax
from jax.experimental import pallas as pl
from jax.experimental.pallas import tpu as pltpu

EPS = 1e-6
POOL_WINDOWS = (2, 4, 8, 16)
POOL_CARRY_ROWS = 16
HEAD_DIM = 128
CONV_WIDTH = 3
LOG2E = math.log2(math.e)
SUBLANES = 8
BF16_ROWS = 16
LANES = 128
W_BLOCK = 256
V7X_VMEM_BYTES = 64 * 1024 * 1024
VMEM_RESERVE_BYTES = 6 * 1024 * 1024

F32 = jnp.float32
BF16 = jnp.bfloat16


def _round_up(x, m):
    return (x + m - 1) // m * m


def _largest_tile(n, cap, quantum):
    if n <= cap:
        return n
    t = cap - cap % quantum
    while t >= quantum:
        if n % t == 0:
            return t
        t -= quantum
    raise ValueError(f"no tile of {n} is a multiple of {quantum} and <= {cap}")


def _params(vmem_estimate_bytes, n_grid_dims):
    limit = min(int(vmem_estimate_bytes * 5 // 4) + VMEM_RESERVE_BYTES,
                V7X_VMEM_BYTES - VMEM_RESERVE_BYTES)
    return pltpu.CompilerParams(
        dimension_semantics=("arbitrary",) * n_grid_dims,
        vmem_limit_bytes=limit,
    )


def _rms(x, g):
    ms = jnp.mean(x * x, axis=-1, keepdims=True)
    return (x * lax.rsqrt(ms + EPS)) * g


def _pool_kernel(h_ref, gm_ref, gf_ref, sc_ref, w_ref, ho_ref, n2_ref, carry_ref):
    s = pl.program_id(1)
    ts, d = h_ref.shape
    c = d // len(POOL_WINDOWS)

    @pl.when(s == 0)
    def _():
        carry_ref[...] = jnp.zeros_like(carry_ref)

    x = h_ref[...]
    n = _rms(x, gm_ref[...])
    pos = s * ts + lax.broadcasted_iota(jnp.int32, (ts, 1), 0)
    for g, w in enumerate(POOL_WINDOWS):
        cols = slice(g * c, (g + 1) * c)
        acc = jnp.concatenate([carry_ref[:, cols], n[:, cols]], axis=0)
        shift = 1
        while shift < w:
            acc = acc + pltpu.roll(acc, shift, 0)
            shift *= 2
        count = jnp.minimum(pos + 1, w).astype(F32)
        mix = (acc[POOL_CARRY_ROWS:, :] / count - n[:, cols]).astype(BF16)
        y = jnp.dot(mix, w_ref[g], preferred_element_type=F32)
        ho_ref[:, cols] = x[:, cols] + y * sc_ref[:, cols]
    carry_ref[...] = n[ts - POOL_CARRY_ROWS:, :]
    n2_ref[...] = _rms(ho_ref[...], gf_ref[...]).astype(n2_ref.dtype)


def _pool_layer(h, g_mix, g_ffn, scale, w_pool):
    b, s, d = h.shape
    ts = _largest_tile(s, 256, POOL_CARRY_ROWS)
    vmem = 2 * ts * d * (4 + 4 + 2) + 2 * w_pool.size * 2 + 8 * ts * d * 4
    row = pl.BlockSpec((None, ts, d), lambda i, j: (i, j, 0))
    vec = pl.BlockSpec((1, d), lambda i, j: (0, 0))
    return pl.pallas_call(
        _pool_kernel,
        grid=(b, s // ts),
        in_specs=[row, vec, vec, vec, pl.BlockSpec(w_pool.shape, lambda i, j: (0, 0, 0))],
        out_specs=[row, row],
        out_shape=[jax.ShapeDtypeStruct(h.shape, F32), jax.ShapeDtypeStruct(h.shape, BF16)],
        scratch_shapes=[pltpu.VMEM((POOL_CARRY_ROWS, d), F32)],
        compiler_params=_params(vmem, 2),
        name="pool_layer",
    )(h, g_mix, g_ffn, scale, w_pool)


def _cast_blocks_kernel(*refs, n_valid):
    *x_refs, o_ref = refs
    j = pl.program_id(1)
    cast = lambda x_ref: jnp.where(j < n_valid, x_ref[...], 0.0).astype(o_ref.dtype)
    if len(x_refs) == 1:
        o_ref[...] = cast(x_refs[0])
    else:
        for t, x_ref in enumerate(x_refs):
            o_ref[t] = cast(x_ref)


def _cast_blocks(xs, grid, in_specs, out_spec, out_shape, n_valid):
    block_elems = sum(math.prod(d for d in sp.block_shape if d is not None) for sp in in_specs)
    return pl.pallas_call(
        functools.partial(_cast_blocks_kernel, n_valid=n_valid),
        grid=grid,
        in_specs=in_specs,
        out_specs=out_spec,
        out_shape=jax.ShapeDtypeStruct(out_shape, BF16),
        compiler_params=_params(2 * block_elems * (4 + 2) + block_elems * 4, 2),
        name="cast_blocks",
    )(*xs)


def _prep_up(w_up, fp):
    n_layers, d, f2 = w_up.shape
    nv, nb = f2 // 2 // W_BLOCK, fp // W_BLOCK
    src = lambda half: pl.BlockSpec(
        (None, d, W_BLOCK), lambda l, j: (l, 0, half * nv + jnp.minimum(j, nv - 1)))
    dst = pl.BlockSpec((None, None, 2, d, W_BLOCK), lambda l, j: (l, j, 0, 0, 0))
    return _cast_blocks([w_up, w_up], (n_layers, nb), [src(0), src(1)], dst,
                        (n_layers, nb, 2, d, W_BLOCK), nv)


def _prep_down(w_down, fp):
    n_layers, f, d = w_down.shape
    nv, nb = f // W_BLOCK, fp // W_BLOCK
    src = pl.BlockSpec((None, W_BLOCK, d), lambda l, j: (l, jnp.minimum(j, nv - 1), 0))
    dst = pl.BlockSpec((None, W_BLOCK, d), lambda l, j: (l, j, 0))
    return _cast_blocks([w_down], (n_layers, nb), [src], dst, (n_layers, fp, d), nv)


def _prep_stack(w):
    n_layers, r, c = w.shape
    tr = _largest_tile(r, 512, BF16_ROWS)
    spec = pl.BlockSpec((None, tr, c), lambda l, j: (l, j, 0))
    return _cast_blocks([w], (n_layers, r // tr), [spec], spec, w.shape, r // tr)


def _prep_rows(w, n_rows):
    c = w.shape[1]
    tr = _largest_tile(n_rows, 512, BF16_ROWS)
    src = pl.BlockSpec((tr, c), lambda l, j: (j, 0))
    dst = pl.BlockSpec((None, tr, c), lambda l, j: (0, j, 0))
    return _cast_blocks([w], (1, n_rows // tr), [src], dst, (1, n_rows, c), n_rows // tr)


def _causal_conv(u, tail, w, b):
    rows = lax.broadcasted_iota(jnp.int32, tail.shape, 0)

    def lagged(k):
        r = pltpu.roll(u, k, 0)
        head = jnp.where(rows < k, pltpu.roll(tail, k, 0), r[:SUBLANES])
        return jnp.concatenate([head, r[SUBLANES:]], axis=0)

    y = b + w[0:1, :] * u
    for k in range(1, CONV_WIDTH):
        y = y + w[k:k + 1, :] * lagged(k)
    return y


def _ffn_up_kernel(a_ref, w_ref, cp_ref, o_ref, *, rc, n_last):
    s = o_ref.shape[0]
    nb, _, _, wb = w_ref.shape
    cw = CONV_WIDTH
    zeros = jnp.zeros((SUBLANES, wb), F32)

    def body(n_blocks):
        tails = [(zeros, zeros)] * n_blocks
        for r0 in range(0, s, rc):
            a = a_ref[r0:r0 + rc, :]
            for t in range(n_blocks):
                cols = slice(t * wb, (t + 1) * wb)
                ug = jnp.dot(a, w_ref[t, 0], preferred_element_type=F32)
                uv = jnp.dot(a, w_ref[t, 1], preferred_element_type=F32)
                cp = cp_ref[:, cols]
                yg = _causal_conv(ug, tails[t][0], cp[0:cw], cp[cw:cw + 1])
                yv = _causal_conv(uv, tails[t][1], cp[cw + 1:2 * cw + 1], cp[2 * cw + 1:2 * cw + 2])
                o_ref[r0:r0 + rc, cols] = (yg * jax.nn.sigmoid(yg) * yv).astype(o_ref.dtype)
                tails[t] = (ug[rc - SUBLANES:, :], uv[rc - SUBLANES:, :])
        if n_blocks < nb:
            o_ref[:, n_blocks * wb:] = jnp.zeros((s, (nb - n_blocks) * wb), o_ref.dtype)

    if n_last == nb:
        body(nb)
    else:
        last = pl.num_programs(1) - 1
        pl.when(pl.program_id(1) < last)(lambda: body(nb))
        pl.when(pl.program_id(1) == last)(lambda: body(n_last))


def _ffn_up(n2, w_up, layer, n_valid, conv_params):
    b, s, d = n2.shape
    fp = conv_params.shape[1]
    nb = 2
    bn = nb * W_BLOCK
    n_last = n_valid - (fp // bn - 1) * nb
    assert 0 < n_last <= nb
    rc = _largest_tile(s, 128, SUBLANES)
    vmem = 2 * s * d * 2 + 4 * d * bn * 2 + 2 * s * bn * 2 + 4 * rc * bn * 4
    return pl.pallas_call(
        functools.partial(_ffn_up_kernel, rc=rc, n_last=n_last),
        grid=(b, fp // bn),
        in_specs=[
            pl.BlockSpec((None, s, d), lambda i, j: (i, 0, 0)),
            pl.BlockSpec((None, nb, 2, d, W_BLOCK), lambda i, j: (layer, j, 0, 0, 0)),
            pl.BlockSpec(conv_params.shape[:1] + (bn,), lambda i, j: (0, j)),
        ],
        out_specs=pl.BlockSpec((None, s, bn), lambda i, j: (i, 0, j)),
        out_shape=jax.ShapeDtypeStruct((b, s, fp), BF16),
        compiler_params=_params(vmem, 2),
        name="ffn_up",
    )(n2, w_up, conv_params)


def _mm_res_norm_kernel(a_ref, w_ref, h_ref, g_ref, *refs, ni, ne, cn, k_last, n_gains, emit_h):
    outs, acc_ref = refs[:-1], refs[-1]
    i, k = pl.program_id(0), pl.program_id(1)
    nk = pl.num_programs(1)
    d = acc_ref.shape[2]
    bm = acc_ref.shape[1] - SUBLANES
    bk = a_ref.shape[1]
    rs = bm // ne

    def finish():
        slot = (i + 1) % 2
        r0 = pl.multiple_of(k * rs, rs)
        hn = acc_ref[slot, pl.ds(r0, rs), :] + h_ref[...]
        if emit_h:
            outs[0][...] = hn
        tag = hn
        if n_gains:
            z = hn * lax.rsqrt(jnp.mean(hn * hn, axis=-1, keepdims=True) + EPS)
            for t in range(n_gains):
                o = outs[t + int(emit_h)]
                o[...] = (z * g_ref[t:t + 1, :]).astype(o.dtype)
            tag = z
        spare = pl.multiple_of(bm + jnp.minimum(k, 0) * SUBLANES, SUBLANES)
        acc_ref[slot, pl.ds(spare, SUBLANES), :] = tag[:SUBLANES, :]

    def accumulate(kw, first, with_finish):
        cur = i % 2
        if with_finish:
            finish()
        for c in range(0, d, cn):
            part = jnp.dot(a_ref[:, :kw], w_ref[:kw, c:c + cn], preferred_element_type=F32)
            if first:
                acc_ref[cur, pl.ds(0, bm), c:c + cn] = part
            else:
                acc_ref[cur, pl.ds(0, bm), c:c + cn] += part

    live = i < ni
    assert nk == 1 or ne < nk or k_last == bk
    if nk == 1:
        pl.when(live & (i == 0))(lambda: accumulate(k_last, True, False))
        pl.when(live & (i > 0))(lambda: accumulate(k_last, True, True))
    else:
        pl.when(live & (k == 0) & (i == 0))(lambda: accumulate(bk, True, False))
        pl.when(live & (k == 0) & (i > 0))(lambda: accumulate(bk, True, True))
        if ne > 1:
            pl.when(live & (k > 0) & (k < ne) & (i == 0))(lambda: accumulate(bk, False, False))
            pl.when(live & (k > 0) & (k < ne) & (i > 0))(lambda: accumulate(bk, False, True))
        if k_last == bk:
            pl.when(live & (k >= ne))(lambda: accumulate(bk, False, False))
        else:
            pl.when(live & (k >= ne) & (k < nk - 1))(lambda: accumulate(bk, False, False))
            pl.when(live & (k == nk - 1))(lambda: accumulate(k_last, False, False))
    pl.when((i == ni) & (k < ne))(finish)


def _mm_res_norm(a, w, layer, h, gains, *, k_valid=None, emit_h=True, norm_dtype=BF16):
    m, kdim = a.shape
    d = w.shape[2]
    n_gains = gains.shape[0]
    bm = _largest_tile(m, 1024, BF16_ROWS)
    bk = _largest_tile(kdim, 512, LANES)
    cn = _largest_tile(d, 512, LANES)
    ni, nk = m // bm, kdim // bk
    k_last = (kdim if k_valid is None else k_valid) - (nk - 1) * bk
    assert 0 < k_last <= bk and k_last % LANES == 0
    ne = 1
    while ne * 2 <= nk and bm // (ne * 2) >= 64:
        ne *= 2
    rs = bm // ne
    norm_bytes = jnp.dtype(norm_dtype).itemsize
    vmem = (2 * bm * d * 4 + 2 * bm * bk * 2 + 2 * bk * d * 2 + 2 * bm * cn * 4
            + rs * d * (2 * 4 + 2 * 4 * int(emit_h) + 2 * norm_bytes * n_gains + 4 * 4))

    def sub_block(i, k):
        return (jnp.maximum(i - 1, 0) * ne + jnp.where(i == 0, 0, jnp.minimum(k, ne - 1)), 0)

    row = pl.BlockSpec((rs, d), sub_block)
    out_shape = ([jax.ShapeDtypeStruct((m, d), F32)] if emit_h else []) + \
        [jax.ShapeDtypeStruct((m, d), norm_dtype)] * n_gains
    g_in = gains if n_gains else jnp.zeros((1, d), F32)
    return pl.pallas_call(
        functools.partial(_mm_res_norm_kernel, ni=ni, ne=ne, cn=cn, k_last=k_last, n_gains=n_gains,
                          emit_h=emit_h),
        grid=(ni + 1, nk),
        in_specs=[
            pl.BlockSpec((bm, bk), lambda i, k: (jnp.minimum(i, ni - 1), jnp.where(i < ni, k, nk - 1))),
            pl.BlockSpec((None, bk, d), lambda i, k: (layer, jnp.where(i < ni, k, nk - 1), 0)),
            row,
            pl.BlockSpec(g_in.shape, lambda i, k: (0, 0)),
        ],
        out_specs=[row] * len(out_shape),
        out_shape=out_shape,
        scratch_shapes=[pltpu.VMEM((2, bm + SUBLANES, d), F32)],
        compiler_params=_params(vmem, 2),
        name="mm_res_norm",
    )(a, w, h, g_in)


def _mm_kernel(a_ref, w_ref, o_ref, *, scale, w_is_transposed):
    dims = (((1,), (1 if w_is_transposed else 0,)), ((), ()))
    acc = lax.dot_general(a_ref[...], w_ref[...], dims, preferred_element_type=F32)
    if scale is not None:
        acc = acc * scale
    o_ref[...] = acc.astype(o_ref.dtype)


def _mm(a, w, layer, scale=None, w_is_transposed=False):
    m, kdim = a.shape
    n = w.shape[1 if w_is_transposed else 2]
    bm = _largest_tile(m, 1024, BF16_ROWS)
    bn = _largest_tile(n, 1024, LANES)
    vmem = 2 * bm * kdim * 2 + 2 * kdim * bn * 2 + 2 * bm * bn * 2 + bm * bn * 4
    return pl.pallas_call(
        functools.partial(_mm_kernel, scale=scale, w_is_transposed=w_is_transposed),
        grid=(m // bm, n // bn),
        in_specs=[pl.BlockSpec((bm, kdim), lambda i, j: (i, 0)),
                  pl.BlockSpec((None, bn, kdim), lambda i, j: (layer, j, 0)) if w_is_transposed
                  else pl.BlockSpec((None, kdim, bn), lambda i, j: (layer, 0, j))],
        out_specs=pl.BlockSpec((bm, bn), lambda i, j: (i, j)),
        out_shape=jax.ShapeDtypeStruct((m, n), BF16),
        compiler_params=_params(vmem, 2),
        name="mm",
    )(a, w)


def _fgate_kernel(a_ref, w_ref, b_ref, o_ref):
    x = jnp.dot(a_ref[...], w_ref[...], preferred_element_type=F32) + b_ref[...]
    c = jnp.minimum(x, 0.0) - jnp.log1p(jnp.exp(-jnp.abs(x)))
    s = c.shape[0]
    rows = lax.broadcasted_iota(jnp.int32, c.shape, 0)
    shift = 1
    while shift < s:
        c = c + jnp.where(rows >= shift, pltpu.roll(c, shift, 0), 0.0)
        shift *= 2
    o_ref[...] = c * LOG2E


def _fgate(n_kv, w_f, b_f):
    b, s, d = n_kv.shape
    hp = w_f.shape[1]
    vmem = 2 * s * d * 2 + 2 * d * hp * 2 + 6 * s * hp * 4
    return pl.pallas_call(
        _fgate_kernel,
        grid=(b,),
        in_specs=[pl.BlockSpec((None, s, d), lambda i: (i, 0, 0)),
                  pl.BlockSpec((d, hp), lambda i: (0, 0)),
                  pl.BlockSpec((1, hp), lambda i: (0, 0))],
        out_specs=pl.BlockSpec((None, s, hp), lambda i: (i, 0, 0)),
        out_shape=jax.ShapeDtypeStruct((b, s, hp), F32),
        compiler_params=_params(vmem, 1),
        name="fgate",
    )(n_kv, w_f, b_f)


def _attn_kernel(q_ref, k_ref, v_ref, cum_ref, ck_ref, o_ref, *, tq):
    s = q_ref.shape[0]
    head_lane = lax.broadcasted_iota(jnp.int32, (tq, cum_ref.shape[1]), 1) == pl.program_id(1)
    nt = (((1,), (1,)), ((), ()))
    neg = jnp.finfo(F32).min
    row = lax.broadcasted_iota(jnp.int32, (tq, tq), 0)
    col = lax.broadcasted_iota(jnp.int32, (tq, tq), 1)
    for i in reversed(range(s // tq)):
        q0, q1 = i * tq, (i + 1) * tq
        q = q_ref[q0:q1, :]
        cq = jnp.sum(jnp.where(head_lane, cum_ref[q0:q1, :], 0.0), axis=-1, keepdims=True)
        diag = lax.dot_general(q, k_ref[q0:q1, :], nt, preferred_element_type=F32)
        diag = jnp.where(col <= row, diag + cq - ck_ref[:, q0:q1], neg)
        m = jnp.max(diag, axis=-1, keepdims=True)
        if i > 0:
            past = lax.dot_general(q, k_ref[0:q0, :], nt, preferred_element_type=F32)
            past = past + cq - ck_ref[:, 0:q0]
            m = jnp.maximum(m, jnp.max(past, axis=-1, keepdims=True))
        p = jnp.exp2(diag - m)
        l = jnp.sum(p, axis=-1, keepdims=True)
        o = jnp.dot(p.astype(BF16), v_ref[q0:q1, :], preferred_element_type=F32)
        if i > 0:
            p = jnp.exp2(past - m)
            l = l + jnp.sum(p, axis=-1, keepdims=True)
            o = o + jnp.dot(p.astype(BF16), v_ref[0:q0, :], preferred_element_type=F32)
        o_ref[q0:q1, :] = (o / l).astype(o_ref.dtype)


def _attention(q, kv, cum, cum_k):
    b, s, d = q.shape
    nh = d // HEAD_DIM
    hp = cum.shape[2]
    tq = _largest_tile(s, 512, LANES)
    vmem = 2 * 4 * s * HEAD_DIM * 2 + 2 * s * hp * 4 + 2 * SUBLANES * s * 4 + 6 * tq * s * 4
    head = lambda i, j: (i, 0, j)
    return pl.pallas_call(
        functools.partial(_attn_kernel, tq=tq),
        grid=(b, nh),
        in_specs=[pl.BlockSpec((None, s, HEAD_DIM), head),
                  pl.BlockSpec((None, s, HEAD_DIM), head),
                  pl.BlockSpec((None, s, HEAD_DIM), lambda i, j: (i, 0, j + nh)),
                  pl.BlockSpec((None, s, hp), lambda i, j: (i, 0, 0)),
                  pl.BlockSpec((None, None, 1, s), lambda i, j: (i, j, 0, 0))],
        out_specs=pl.BlockSpec((None, s, HEAD_DIM), head),
        out_shape=jax.ShapeDtypeStruct((b, s, d), BF16),
        compiler_params=_params(vmem, 2),
        name="attention",
    )(q, kv, kv, cum, cum_k)


def _pad_last(x, n):
    return jnp.pad(x, [(0, 0)] * (x.ndim - 1) + [(0, n - x.shape[-1])])


def kernel(x, ln_mix, ln_ffn, pool_w, pool_scale, kv_norm, w_kvf, b_f, w_q, w_o, w_up, conv_w,
           conv_b, w_down, final_norm):
    b, s, d = x.shape
    m = b * s
    depth = ln_mix.shape[0]
    n_a = pool_w.shape[0]
    nh = d // HEAD_DIM
    f = w_down.shape[1]
    fp = _round_up(f, 2 * W_BLOCK)
    hp = _round_up(nh, LANES)

    w_up_b = _prep_up(w_up, fp)
    w_dn = _prep_down(w_down, fp)
    w_kvf_t = w_kvf.T
    w_kv_t = _prep_rows(w_kvf_t, 2 * d)
    w_q_b, w_o_b, pool_w_b = _prep_stack(w_q), _prep_stack(w_o), pool_w.astype(BF16)
    halves = lambda p: (_pad_last(p[..., :f], fp), _pad_last(p[..., f:], fp))

    def ffn(h2d, n2, layer, gains, **kw):
        cw_gate, cw_val = halves(conv_w[layer])
        cb_gate, cb_val = halves(conv_b[layer][None])
        conv_params = jnp.concatenate([cw_gate, cb_gate, cw_val, cb_val], axis=0)
        act = _ffn_up(n2.reshape(b, s, d), w_up_b, layer, f // W_BLOCK, conv_params)
        return _mm_res_norm(act.reshape(m, fp), w_dn, layer, h2d, gains, k_valid=f, **kw)

    h = x
    outs = None
    for layer in range(n_a):
        h, n2 = _pool_layer(h, ln_mix[layer][None], ln_ffn[layer][None], pool_scale[layer][None],
                            pool_w_b[layer])
        if layer + 1 < n_a:
            gains = jnp.zeros((0, d), F32)
        else:
            gains = jnp.stack([kv_norm, ln_mix[n_a]])
        outs = ffn(h.reshape(m, d), n2, layer, gains)
        h = outs[0].reshape(b, s, d)

    h2d, n_kv, n_q = outs
    kv = _mm(n_kv, w_kv_t, 0, w_is_transposed=True).reshape(b, s, 2 * d)
    cum = _fgate(n_kv.reshape(b, s, d), _pad_last(w_kvf[:, 2 * d:], hp).astype(BF16),
                 _pad_last(b_f[None, :], hp))
    cum_k = cum[:, :, :nh].transpose(0, 2, 1)[:, :, None, :]

    for j in range(depth - n_a):
        layer = n_a + j
        q = _mm(n_q, w_q_b, j, scale=HEAD_DIM ** -0.5 * LOG2E).reshape(b, s, d)
        o = _attention(q, kv, cum, cum_k)
        h2d, n2 = _mm_res_norm(o.reshape(m, d), w_o_b, j, h2d, ln_ffn[layer][None])
        if layer + 1 < depth:
            h2d, n_q = ffn(h2d, n2, layer, ln_mix[layer + 1][None])
        else:
            (out,) = ffn(h2d, n2, layer, final_norm[None], emit_h=False, norm_dtype=F32)
    return out.reshape(b, s, d)
```
